```python
import jax, jax.numpy as jnp
from jax import lax
import numpy as np

D_MODEL = 2048
BATCH = 4
SEQ = 2048
DEPTH = 4
DEC_BATCH = 8
DEC_SEQ = 1
PAST_LEN = 16384
PAGE_SIZE = 128

M_HEADS = 4
M_QK_DIM = 128
M_V_DIM = 256
M_CHUNK = 64
FORGET_BIAS = 3.0
N_HEADS = 8
HEAD_DIM = 128
IDX_HEADS = 16
IDX_DIM = 64
TOPK_MAX = 256
Q_BLOCK = 64
ROPE_THETA = 500000.0
D_FF = 5632
EPS = 1e-6

M_QK = M_HEADS * M_QK_DIM
M_V = M_HEADS * M_V_DIM
A_W = N_HEADS * HEAD_DIM
IDX_QW = IDX_HEADS * IDX_DIM
SPLIT_SIZES = (M_QK, M_QK, M_V, M_V, M_HEADS, M_HEADS, A_W, A_W, A_W, IDX_QW, IDX_DIM, IDX_HEADS, D_MODEL, D_MODEL)
SPLIT_POINTS = tuple(sum(SPLIT_SIZES[:i + 1]) for i in range(len(SPLIT_SIZES) - 1))
N_IN = sum(SPLIT_SIZES)
F_GATE_START = sum(SPLIT_SIZES[:5])

kernel_name = 'hybrid_mlstm_dsa_macaron_step'


def rmsnorm(x, g):
    xf = x.astype(jnp.float32)
    y = xf * lax.rsqrt(jnp.mean(xf * xf, axis=-1, keepdims=True) + EPS)
    return (y * g.astype(jnp.float32)).astype(x.dtype)


def rope(x, pos):
    rd = x.shape[-1] // 4
    half = rd // 2
    inv = ROPE_THETA ** (-jnp.arange(half, dtype=jnp.float32) * (2.0 / rd))
    ang = pos.astype(jnp.float32)[:, None] * inv[None, :]
    cos = jnp.cos(ang)[None, :, None, :]
    sin = jnp.sin(ang)[None, :, None, :]
    xr = x[..., :rd].astype(jnp.float32)
    x1, x2 = xr[..., :half], xr[..., half:]
    rot = jnp.concatenate([x1 * cos - x2 * sin, x2 * cos + x1 * sin], axis=-1).astype(x.dtype)
    return jnp.concatenate([rot, x[..., rd:]], axis=-1)


def half_ffn(x, g, w_gu, w_down):
    h = rmsnorm(x, g)
    a, u = jnp.split(h @ w_gu, 2, axis=-1)
    return x + 0.5 * ((jax.nn.silu(a) * u) @ w_down)


def branch_inputs(x, pos, g_mix, w_in, b_in, g_q, g_k, g_ik):
    B, T, _ = x.shape
    z = rmsnorm(x, g_mix) @ w_in + b_in
    (mq, mk, mv, mo, mi, mf, aq, ak, av, iq, ik, iw, ga, gb) = jnp.split(z, SPLIT_POINTS, axis=-1)
    mq = mq.reshape(B, T, M_HEADS, M_QK_DIM) * (M_QK_DIM ** -0.5)
    mk = mk.reshape(B, T, M_HEADS, M_QK_DIM)
    mv = mv.reshape(B, T, M_HEADS, M_V_DIM)
    log_i = mi.astype(jnp.float32)
    log_f = jax.nn.log_sigmoid(mf.astype(jnp.float32))
    aq = rope(rmsnorm(aq.reshape(B, T, N_HEADS, HEAD_DIM), g_q), pos)
    ak = rope(rmsnorm(ak.reshape(B, T, N_HEADS, HEAD_DIM), g_k), pos)
    av = av.reshape(B, T, N_HEADS, HEAD_DIM)
    iq = rope(iq.reshape(B, T, IDX_HEADS, IDX_DIM), pos)
    ik = rope(rmsnorm(ik.reshape(B, T, 1, IDX_DIM), g_ik), pos)[:, :, 0]
    iw = iw * (IDX_HEADS ** -0.5)
    return mq, mk, mv, mo, log_i, log_f, aq, ak, av, iq, ik, iw, ga, gb


def merge_branches(x, h_m, mo, att, ga, gb, g_m, w_a, w_b, w_out):
    B, T, _ = x.shape
    hm = rmsnorm(h_m, g_m) * jax.nn.sigmoid(mo).reshape(B, T, M_HEADS, M_V_DIM)
    br_a = hm.reshape(B, T, M_V) @ w_a
    br_b = att.reshape(B, T, A_W) @ w_b
    return x + (jax.nn.sigmoid(ga) * br_a + jax.nn.sigmoid(gb) * br_b) @ w_out


def mlstm_chunk(state, inp):
    S0, n0, m0 = state
    q, k, v, log_i, log_f = (a.astype(jnp.float32) for a in inp)
    T = q.shape[1]
    b = jnp.cumsum(log_f, axis=1)
    causal = jnp.tril(jnp.ones((T, T), dtype=bool))
    d = b[:, :, None, :] - b[:, None, :, :] + log_i[:, None, :, :]
    d = jnp.where(causal[None, :, :, None], d, -jnp.inf)
    inter = b + m0[:, None, :]
    m = jnp.maximum(inter, jnp.max(d, axis=2))
    w = jnp.einsum('bthd,bshd->btsh', q, k) * jnp.exp(d - m[:, :, None, :])
    c_inter = jnp.exp(inter - m)
    num = c_inter[..., None] * jnp.einsum('bthd,bhde->bthe', q, S0) + jnp.einsum('btsh,bshe->bthe', w, v)
    den = c_inter * jnp.einsum('bthd,bhd->bth', q, n0) + jnp.sum(w, axis=2)
    h = num / jnp.maximum(jnp.abs(den), jnp.exp(-m))[..., None]
    b_end = b[:, -1]
    g = b_end[:, None, :] - b + log_i
    m_new = jnp.maximum(b_end + m0, jnp.max(g, axis=1))
    a = jnp.exp(g - m_new[:, None, :])
    c_old = jnp.exp(b_end + m0 - m_new)
    S_new = c_old[..., None, None] * S0 + jnp.einsum('bsh,bshd,bshe->bhde', a, k, v)
    n_new = c_old[..., None] * n0 + jnp.einsum('bsh,bshd->bhd', a, k)
    return (S_new, n_new, m_new), h


def mlstm_prompt(q, k, v, log_i, log_f):
    B, S, H, _ = q.shape
    nc = S // M_CHUNK

    def to_chunks(a):
        return jnp.moveaxis(a.reshape(B, nc, M_CHUNK, *a.shape[2:]), 1, 0)

    init = (jnp.zeros((B, H, M_QK_DIM, M_V_DIM), jnp.float32),
            jnp.zeros((B, H, M_QK_DIM), jnp.float32),
            jnp.zeros((B, H), jnp.float32))
    state, h = lax.scan(mlstm_chunk, init, tuple(to_chunks(a) for a in (q, k, v, log_i, log_f)))
    h = jnp.moveaxis(h, 0, 1).reshape(B, S, H, M_V_DIM)
    return h.astype(q.dtype), state


def mlstm_step(q, k, v, log_i, log_f, C0, n0, m0):
    state = (C0.astype(jnp.float32), n0.astype(jnp.float32), m0.astype(jnp.float32))
    state, h = mlstm_chunk(state, (q, k, v, log_i, log_f))
    return h.astype(q.dtype), state


def indexer_scores(qi, ki, wi):
    s = jax.nn.relu(jnp.einsum('bqhd,bld->bqhl', qi, ki).astype(jnp.float32) * (IDX_DIM ** -0.5))
    return jnp.einsum('bqhl,bqh->bql', s, wi.astype(jnp.float32))


def sparse_attend(q, kg, vg, valid):
    s = jnp.einsum('bqhd,bqkhd->bqhk', q, kg).astype(jnp.float32) * (HEAD_DIM ** -0.5)
    s = jnp.where(valid[:, :, None, :], s, -jnp.inf)
    p = jax.nn.softmax(s, axis=-1)
    return jnp.einsum('bqhk,bqkhd->bqhd', p.astype(vg.dtype), vg)


gather_rows = jax.vmap(lambda a, i: a[i])


def dsa_prompt(q, k, v, qi, ki, wi):
    B, S, H, dh = q.shape
    topk = min(TOPK_MAX, S // 4)
    key_pos = jnp.arange(S)

    def block(start):
        qb = lax.dynamic_slice_in_dim(q, start, Q_BLOCK, axis=1)
        qib = lax.dynamic_slice_in_dim(qi, start, Q_BLOCK, axis=1)
        wib = lax.dynamic_slice_in_dim(wi, start, Q_BLOCK, axis=1)
        qpos = start + jnp.arange(Q_BLOCK)
        score = indexer_scores(qib, ki, wib)
        score = jnp.where((key_pos[None, :] <= qpos[:, None])[None], score, -jnp.inf)
        _, idx = lax.top_k(score, topk)
        valid = idx <= qpos[None, :, None]
        return sparse_attend(qb, gather_rows(k, idx), gather_rows(v, idx), valid)

    out = lax.map(block, jnp.arange(0, S, Q_BLOCK))
    return jnp.moveaxis(out, 0, 1).reshape(B, S, H, dh)


def dsa_sample(q, k, v, qi, ki, wi, cache_k, cache_v, cache_kidx, page_table, layer):
    B, T, H, dh = q.shape
    past = page_table.shape[1] * PAGE_SIZE
    L = past + T
    topk = min(TOPK_MAX, L // 4)
    ki_past = cache_kidx[layer, page_table].reshape(B, past, IDX_DIM)
    ki_all = jnp.concatenate([ki_past, ki.astype(ki_past.dtype)], axis=1)
    qpos = past + jnp.arange(T)
    score = indexer_scores(qi, ki_all, wi)
    score = jnp.where((jnp.arange(L)[None, :] <= qpos[:, None])[None], score, -jnp.inf)
    _, idx = lax.top_k(score, topk)
    valid = idx <= qpos[None, :, None]
    in_past = (idx < past)[..., None, None]
    pidx = jnp.minimum(idx, past - 1)
    phys = jax.vmap(lambda pt, i: pt[i // PAGE_SIZE])(page_table, pidx)
    slot = pidx % PAGE_SIZE
    nidx = jnp.clip(idx - past, 0, T - 1)
    kg = jnp.where(in_past, cache_k[layer, phys, slot], gather_rows(k, nidx))
    vg = jnp.where(in_past, cache_v[layer, phys, slot], gather_rows(v, nidx))
    return sparse_attend(q, kg, vg, valid)


def trunk_layer(x, pos, mlstm_fn, dsa_fn, p):
    (g_f1, w_f1_gu, w_f1_dn, g_mix, w_in_l, b_in_l, g_q, g_k, g_ik, g_m,
     w_a, w_b, w_o, g_f2, w_f2_gu, w_f2_dn) = p
    x = half_ffn(x, g_f1, w_f1_gu, w_f1_dn)
    (mq, mk, mv, mo, li, lf, aq, ak, av, iq, ik, iw, ga, gb) = branch_inputs(x, pos, g_mix, w_in_l, b_in_l, g_q, g_k, g_ik)
    h_m, (C, n, m) = mlstm_fn(mq, mk, mv, li, lf)
    att = dsa_fn(aq, ak, av, iq, ik, iw)
    x = merge_branches(x, h_m, mo, att, ga, gb, g_m, w_a, w_b, w_o)
    x = half_ffn(x, g_f2, w_f2_gu, w_f2_dn)
    return x, (ak, av, ik, C, n, m)


def setup_inputs(seed: int = 0) -> dict:
    key = jax.random.key(seed)
    ks = jax.random.split(key, 32)

    def nrm(k, shape, scale=1.0):
        return jax.random.normal(k, shape, jnp.float32) * scale

    n_pages = PAST_LEN // PAGE_SIZE
    n_used = DEC_BATCH * n_pages
    n_pool = n_used + max(1, n_used // 4)
    page_table = jax.random.permutation(ks[9], n_pool)[:n_used].reshape(DEC_BATCH, n_pages).astype(jnp.int32)
    b_in = nrm(ks[15], (DEPTH, N_IN), 0.02)
    b_in = b_in.at[:, F_GATE_START:F_GATE_START + M_HEADS].add(FORGET_BIAS)
    return {
        'x_prompt': nrm(ks[0], (BATCH, SEQ, D_MODEL)),
        'x_sample': nrm(ks[1], (DEC_BATCH, DEC_SEQ, D_MODEL)),
        'cache_k': nrm(ks[2], (DEPTH, n_pool, PAGE_SIZE, N_HEADS, HEAD_DIM)),
        'cache_v': nrm(ks[3], (DEPTH, n_pool, PAGE_SIZE, N_HEADS, HEAD_DIM)),
        'cache_kidx': nrm(ks[4], (DEPTH, n_pool, PAGE_SIZE, IDX_DIM)),
        'state_C': nrm(ks[5], (DEPTH, DEC_BATCH, M_HEADS, M_QK_DIM, M_V_DIM)),
        'state_n': nrm(ks[6], (DEPTH, DEC_BATCH, M_HEADS, M_QK_DIM)),
        'state_m': nrm(ks[7], (DEPTH, DEC_BATCH, M_HEADS), 0.5),
        'page_table': page_table,
        'norm_ffn1': 1.0 + nrm(ks[10], (DEPTH, D_MODEL), 0.02),
        'w_ffn1_gu': nrm(ks[11], (DEPTH, D_MODEL, 2 * D_FF), D_MODEL ** -0.5),
        'w_ffn1_down': nrm(ks[12], (DEPTH, D_FF, D_MODEL), D_FF ** -0.5),
        'norm_mix': 1.0 + nrm(ks[13], (DEPTH, D_MODEL), 0.02),
        'w_in': nrm(ks[14], (DEPTH, D_MODEL, N_IN), D_MODEL ** -0.5),
        'b_in': b_in,
        'q_norm': 1.0 + nrm(ks[16], (DEPTH, HEAD_DIM), 0.02),
        'k_norm': 1.0 + nrm(ks[17], (DEPTH, HEAD_DIM), 0.02),
        'idx_k_norm': 1.0 + nrm(ks[18], (DEPTH, IDX_DIM), 0.02),
        'mlstm_norm': 1.0 + nrm(ks[19], (DEPTH, M_HEADS, M_V_DIM), 0.02),
        'w_branch_a': nrm(ks[20], (DEPTH, M_V, D_MODEL), M_V ** -0.5),
        'w_branch_b': nrm(ks[21], (DEPTH, A_W, D_MODEL), A_W ** -0.5),
        'w_out': nrm(ks[22], (DEPTH, D_MODEL, D_MODEL), D_MODEL ** -0.5),
        'norm_ffn2': 1.0 + nrm(ks[23], (DEPTH, D_MODEL), 0.02),
        'w_ffn2_gu': nrm(ks[24], (DEPTH, D_MODEL, 2 * D_FF), D_MODEL ** -0.5),
        'w_ffn2_down': nrm(ks[25], (DEPTH, D_FF, D_MODEL), D_FF ** -0.5),
    }


def reference(x_prompt, x_sample, cache_k, cache_v, cache_kidx, state_C, state_n, state_m, page_table,
              norm_ffn1, w_ffn1_gu, w_ffn1_down, norm_mix, w_in, b_in, q_norm, k_norm, idx_k_norm,
              mlstm_norm, w_branch_a, w_branch_b, w_out, norm_ffn2, w_ffn2_gu, w_ffn2_down):
    pos_p = jnp.arange(x_prompt.shape[1])
    pos_s = PAST_LEN + jnp.arange(x_sample.shape[1])
    xp, xs = x_prompt, x_sample
    new_p, new_s = [], []
    for l in range(DEPTH):
        params = (norm_ffn1[l], w_ffn1_gu[l], w_ffn1_down[l], norm_mix[l], w_in[l], b_in[l],
                  q_norm[l], k_norm[l], idx_k_norm[l], mlstm_norm[l], w_branch_a[l], w_branch_b[l],
                  w_out[l], norm_ffn2[l], w_ffn2_gu[l], w_ffn2_down[l])
        xp, rows_p = trunk_layer(xp, pos_p, mlstm_prompt, dsa_prompt, params)
        xs, rows_s = trunk_layer(
            xs, pos_s,
            lambda q, k, v, li, lf, l=l: mlstm_step(q, k, v, li, lf, state_C[l], state_n[l], state_m[l]),
            lambda q, k, v, qi, ki, wi, l=l: dsa_sample(q, k, v, qi, ki, wi, cache_k, cache_v, cache_kidx, page_table, l),
            params)
        new_p.append(rows_p)
        new_s.append(rows_s)
    k_prompt = jnp.stack([r[0] for r in new_p])
    v_prompt = jnp.stack([r[1] for r in new_p])
    kidx_prompt = jnp.stack([r[2] for r in new_p])
    C_prompt = jnp.stack([r[3] for r in new_p])
    n_prompt = jnp.stack([r[4] for r in new_p])
    m_prompt = jnp.stack([r[5] for r in new_p])
    k_sample = jnp.stack([r[0] for r in new_s])
    v_sample = jnp.stack([r[1] for r in new_s])
    kidx_sample = jnp.stack([r[2] for r in new_s])
    C_sample = jnp.stack([r[3] for r in new_s])
    n_sample = jnp.stack([r[4] for r in new_s])
    m_sample = jnp.stack([r[5] for r in new_s])
    return (xp, xs, k_prompt, v_prompt, kidx_prompt, C_prompt, n_prompt, m_prompt,
            k_sample, v_sample, kidx_sample, C_sample, n_sample, m_sample)
```

```python
import functools

import jax
import jax.numpy as jnp
import numpy as np
from jax import lax
from jax.experimental import pallas as pl
from jax.experimental.pallas import tpu as pltpu

F32 = jnp.float32
BF16 = jnp.bfloat16
I32 = jnp.int32

EPS = 1e-6
ROPE_THETA = 500000.0
PAGE_SIZE = 128
TOPK_MAX = 256

M_HEADS = 4
M_QK_DIM = 128
M_V_DIM = 256
N_HEADS = 8
HEAD_DIM = 128
IDX_HEADS = 16
IDX_DIM = 64
D_MODEL = 2048

M_QK = M_HEADS * M_QK_DIM
M_V = M_HEADS * M_V_DIM
A_W = N_HEADS * HEAD_DIM
IDX_QW = IDX_HEADS * IDX_DIM

C_GA = 0
C_GB = C_GA + D_MODEL
C_MQ = C_GB + D_MODEL
C_MK = C_MQ + M_QK
C_MV = C_MK + M_QK
C_MO = C_MV + M_V
C_AQ = C_MO + M_V
C_AK = C_AQ + A_W
C_AV = C_AK + A_W
C_IQ = C_AV + A_W
C_MISC = C_IQ + IDX_QW
L_IK = 0
L_IW = IDX_DIM
L_MI = L_IW + IDX_HEADS
L_MF = L_MI + M_HEADS
MISC_W = 128
NZ = 11520

VMEM_LIMIT = 58 * 1024 * 1024

NEG_BIG = -1e30


def _cparams(sem, vmem=VMEM_LIMIT):
    return pltpu.CompilerParams(dimension_semantics=sem, vmem_limit_bytes=vmem)


def _rms(x, g):
    return x * lax.rsqrt(jnp.mean(x * x, axis=-1, keepdims=True) + EPS) * g


def _dot(a, b):
    return jnp.dot(a, b, preferred_element_type=F32)


def _dot_nt(a, b):
    return lax.dot_general(a, b, (((1,), (1,)), ((), ())), preferred_element_type=F32)


def _log_sigmoid(x):
    return jnp.minimum(x, 0.0) - jnp.log1p(jnp.exp(-jnp.abs(x)))


def _ffn_body(x_ref, g_ref, wg_ref, wu_ref, wd_ref, o_ref, h_ref):
    @pl.when(pl.program_id(1) == 0)
    def _():
        x = x_ref[...]
        h_ref[...] = _rms(x, g_ref[...]).astype(BF16)
        o_ref[...] = x

    h = h_ref[...]
    a = _dot(h, wg_ref[...].astype(BF16))
    u = _dot(h, wu_ref[...].astype(BF16))
    p = (0.5 * a) * jax.nn.sigmoid(a) * u
    o_ref[...] += _dot(p.astype(BF16), wd_ref[...].astype(BF16))


def ffn(x, g_all, wgu_all, wdn_all, layer, *, tm, tf):
    M, D = x.shape
    F = wdn_all.shape[1]
    nf = F // tf
    return pl.pallas_call(
        _ffn_body,
        grid=(M // tm, nf),
        in_specs=[
            pl.BlockSpec((tm, D), lambda i, j: (i, 0)),
            pl.BlockSpec((None, 1, D), lambda i, j: (layer, 0, 0)),
            pl.BlockSpec((None, D, tf), lambda i, j: (layer, 0, j)),
            pl.BlockSpec((None, D, tf), lambda i, j: (layer, 0, j + nf)),
            pl.BlockSpec((None, tf, D), lambda i, j: (layer, j, 0)),
        ],
        out_specs=pl.BlockSpec((tm, D), lambda i, j: (i, 0)),
        out_shape=jax.ShapeDtypeStruct((M, D), F32),
        scratch_shapes=[pltpu.VMEM((tm, D), BF16)],
        compiler_params=_cparams(("parallel", "arbitrary")),
        name="ffn",
    )(x, g_all, wgu_all, wgu_all, wdn_all)


def _inproj_body(x_ref, g_ref, w_ref, b_ref, z_ref, h_ref):
    @pl.when(pl.program_id(1) == 0)
    def _():
        h_ref[...] = _rms(x_ref[...], g_ref[...]).astype(BF16)

    z_ref[...] = _dot(h_ref[...], w_ref[...]) + b_ref[...]


def inproj(x, g_all, wr_all, br_all, layer, *, tm, tn):
    M, D = x.shape
    return pl.pallas_call(
        _inproj_body,
        grid=(M // tm, NZ // tn),
        in_specs=[
            pl.BlockSpec((tm, D), lambda i, j: (i, 0)),
            pl.BlockSpec((None, 1, D), lambda i, j: (layer, 0, 0)),
            pl.BlockSpec((None, D, tn), lambda i, j: (layer, 0, j)),
            pl.BlockSpec((None, 1, tn), lambda i, j: (layer, 0, j)),
        ],
        out_specs=pl.BlockSpec((tm, tn), lambda i, j: (i, j)),
        out_shape=jax.ShapeDtypeStruct((M, NZ), F32),
        scratch_shapes=[pltpu.VMEM((tm, D), BF16)],
        compiler_params=_cparams(("parallel", "arbitrary")),
        name="inproj",
    )(x, g_all, wr_all, br_all)


def _rope(x, c, sa, sb, shift):
    n = x.shape[-1]
    return x * c + pltpu.roll(x, n - shift, 1) * sa + pltpu.roll(x, shift, 1) * sb


def _qkpost_body(aq_ref, ak_ref, av_ref, iq_ref, misc_ref, gq_ref, gk_ref, gik_ref,
                 c128_ref, sa128_ref, sb128_ref, c64_ref, sa64_ref, sb64_ref,
                 qb_ref, kf_ref, kb_ref, vb_ref, iqb_ref, miscr_ref):
    c128, sa128, sb128 = c128_ref[...], sa128_ref[...], sb128_ref[...]
    c64, sa64, sb64 = c64_ref[...], sa64_ref[...], sb64_ref[...]
    gq, gk = gq_ref[...], gk_ref[...]
    half128 = HEAD_DIM // 8
    half64 = IDX_DIM // 8
    for h in range(N_HEADS):
        sl = slice(h * HEAD_DIM, (h + 1) * HEAD_DIM)
        q = _rope(_rms(aq_ref[:, sl], gq), c128, sa128, sb128, half128)
        qb_ref[:, sl] = (q * (HEAD_DIM ** -0.5)).astype(BF16)
        k = _rope(_rms(ak_ref[:, sl], gk), c128, sa128, sb128, half128)
        kf_ref[:, sl] = k
        kb_ref[:, sl] = k.astype(BF16)
    vb_ref[...] = av_ref[...].astype(BF16)
    for h in range(IDX_QW // 128):
        sl = slice(h * 128, (h + 1) * 128)
        iqb_ref[:, sl] = _rope(iq_ref[:, sl], c64, sa64, sb64, half64).astype(BF16)
    misc = misc_ref[...]
    lane = lax.broadcasted_iota(I32, misc.shape, 1)
    is_ik = lane < IDX_DIM
    ikv = jnp.where(is_ik, misc, 0.0)
    ms = jnp.sum(ikv * ikv, axis=-1, keepdims=True) * (1.0 / IDX_DIM)
    ikn = ikv * lax.rsqrt(ms + EPS) * gik_ref[...]
    ikr = _rope(ikn, c64, sa64, sb64, half64)
    is_iw = (lane >= L_IW) & (lane < L_IW + IDX_HEADS)
    iw_scale = (IDX_HEADS ** -0.5) * (IDX_DIM ** -0.5)
    miscr_ref[...] = jnp.where(is_ik, ikr, jnp.where(is_iw, misc * iw_scale, misc))


def qkpost(z, gq_all, gk_all, gik_all, tabs, layer, *, tm, tab_blocks):
    M = z.shape[0]
    zspec = lambda w, c: pl.BlockSpec((tm, w), lambda i: (i, c // w))
    tspec = pl.BlockSpec((tm, 128), lambda i: (i % tab_blocks, 0))
    gspec = pl.BlockSpec((None, 1, 128), lambda i: (layer, 0, 0))
    row = lambda w: pl.BlockSpec((tm, w), lambda i: (i, 0))
    return pl.pallas_call(
        _qkpost_body,
        grid=(M // tm,),
        in_specs=[zspec(A_W, C_AQ), zspec(A_W, C_AK), zspec(A_W, C_AV), zspec(IDX_QW, C_IQ),
                  zspec(MISC_W, C_MISC), gspec, gspec, gspec] + [tspec] * 6,
        out_specs=[row(A_W), row(A_W), row(A_W), row(A_W), row(IDX_QW), row(MISC_W)],
        out_shape=[jax.ShapeDtypeStruct((M, A_W), BF16), jax.ShapeDtypeStruct((M, A_W), F32),
                   jax.ShapeDtypeStruct((M, A_W), BF16), jax.ShapeDtypeStruct((M, A_W), BF16),
                   jax.ShapeDtypeStruct((M, IDX_QW), BF16), jax.ShapeDtypeStruct((M, MISC_W), F32)],
        compiler_params=_cparams(("parallel",)),
        name="qkpost",
    )(z, z, z, z, z, gq_all, gk_all, gik_all, *tabs)


def rope_tables(pos):
    pos = pos.astype(F32)[:, None]

    def tables(width):
        rd = width // 4
        half = rd // 2
        inv = ROPE_THETA ** (-jnp.arange(half, dtype=F32) * (2.0 / rd))
        lane = np.arange(128) % width
        ang = pos * inv[None, :][:, lane % half]
        cos, sin = jnp.cos(ang), jnp.sin(ang)
        lo = jnp.asarray(lane < half)[None, :]
        hi = jnp.asarray((lane >= half) & (lane < rd))[None, :]
        c = jnp.where(lo | hi, cos, 1.0)
        sa = jnp.where(lo, -sin, 0.0)
        sb = jnp.where(hi, sin, 0.0)
        return c, sa, sb

    return tables(HEAD_DIM) + tables(IDX_DIM)


def _split3(x):
    hi = x.astype(BF16)
    r1 = x - hi.astype(F32)
    mid = r1.astype(BF16)
    lo = (r1 - mid.astype(F32)).astype(BF16)
    return hi, mid, lo


def _mlstm_body(zq_ref, zk_ref, zv_ref, zo_ref, misc_ref, gm_ref, s0_ref, hm_ref, sout_ref, s_ref, m_ref):
    c = pl.program_id(1)
    T = zq_ref.shape[0]
    EW = s_ref.shape[-1]

    @pl.when(c == 0)
    def _():
        s_ref[...] = s0_ref[...]
        for h in range(M_HEADS):
            m_ref[h:h + 1, :] = jnp.broadcast_to(s0_ref[h, 0:1, M_V_DIM + 1:M_V_DIM + 2], (1, 128))

    misc = misc_ref[...]
    misc_t = misc.T
    gates_t = misc_t[L_MI:L_MI + 8, :]
    lf_rows = _log_sigmoid(gates_t)
    lf_cols = _log_sigmoid(misc)
    ti = lax.broadcasted_iota(I32, (T, T), 0)
    si = lax.broadcasted_iota(I32, (T, T), 1)
    causal = si <= ti
    ltri = jnp.where(causal, 1.0, 0.0).astype(BF16)
    utri = jnp.where(ti <= si, 1.0, 0.0).astype(BF16)
    b_cols = sum(_dot(ltri, p) for p in _split3(lf_cols))
    b_rows = sum(_dot(p, utri) for p in _split3(lf_rows))
    ones_col = jnp.where(lax.broadcasted_iota(I32, (T, EW - M_V_DIM), 1) == 0, 1.0, 0.0).astype(BF16)

    for h in range(M_HEADS):
        q = (zq_ref[:, h * M_QK_DIM:(h + 1) * M_QK_DIM] * (M_QK_DIM ** -0.5)).astype(BF16)
        k = zk_ref[:, h * M_QK_DIM:(h + 1) * M_QK_DIM]
        ve = jnp.concatenate([zv_ref[:, h * M_V_DIM:(h + 1) * M_V_DIM].astype(BF16), ones_col], axis=1)
        bcol = b_cols[:, L_MF + h:L_MF + h + 1]
        brow = b_rows[M_HEADS + h:M_HEADS + h + 1, :]
        r = gates_t[h:h + 1, :] - brow
        m0 = m_ref[h:h + 1, 0:1]
        d = jnp.where(causal, bcol + r, -jnp.inf)
        inter = bcol + m0
        m = jnp.maximum(inter, jnp.max(d, axis=1, keepdims=True))
        w = (_dot_nt(q, k.astype(BF16)) * jnp.exp(d - m)).astype(BF16)
        c_inter = jnp.exp(inter - m)
        s0 = s_ref[h]
        nd = c_inter * _dot(q, s0.astype(BF16)) + _dot(w, ve)
        hh = nd[:, :M_V_DIM] / jnp.maximum(jnp.abs(nd[:, M_V_DIM:M_V_DIM + 1]), jnp.exp(-m))
        sl = slice(h * M_V_DIM, (h + 1) * M_V_DIM)
        hm_ref[:, sl] = (_rms(hh, gm_ref[:, sl]) * jax.nn.sigmoid(zo_ref[:, sl])).astype(BF16)

        b_end = brow[:, T - 1:T]
        g = b_end + r
        m_new = jnp.maximum(b_end + m0, jnp.max(g, axis=1, keepdims=True))
        a = jnp.exp(g - m_new)
        c_old = jnp.exp(b_end + m0 - m_new)
        s_ref[h] = c_old * s0 + _dot((k.T * a).astype(BF16), ve)
        m_ref[h:h + 1, :] = jnp.broadcast_to(m_new, (1, 128))

    @pl.when(c == pl.num_programs(1) - 1)
    def _():
        ri = lax.broadcasted_iota(I32, s_ref.shape[1:], 0)
        ci = lax.broadcasted_iota(I32, s_ref.shape[1:], 1)
        for h in range(M_HEADS):
            sout_ref[h] = jnp.where((ri == 0) & (ci == M_V_DIM + 1), m_ref[h:h + 1, 0:1], s_ref[h])


def mlstm_prompt(z, gm_all, layer, *, B, S, T):
    EW = M_V_DIM + 128
    nc = S // T
    s0 = jnp.zeros((M_HEADS, M_QK_DIM, EW), F32)
    zspec = lambda w, c0: pl.BlockSpec((T, w), lambda b, c: (b * nc + c, c0 // w))
    hm, sout = pl.pallas_call(
        _mlstm_body,
        grid=(B, nc),
        in_specs=[zspec(M_QK, C_MQ), zspec(M_QK, C_MK), zspec(M_V, C_MV), zspec(M_V, C_MO), zspec(MISC_W, C_MISC),
                  pl.BlockSpec((None, 1, M_V), lambda b, c: (layer, 0, 0)),
                  pl.BlockSpec((M_HEADS, M_QK_DIM, EW), lambda b, c: (0, 0, 0))],
        out_specs=[pl.BlockSpec((T, M_V), lambda b, c: (b * nc + c, 0)),
                   pl.BlockSpec((None, M_HEADS, M_QK_DIM, EW), lambda b, c: (b, 0, 0, 0))],
        out_shape=[jax.ShapeDtypeStruct((B * S, M_V), BF16),
                   jax.ShapeDtypeStruct((B, M_HEADS, M_QK_DIM, EW), F32)],
        scratch_shapes=[pltpu.VMEM((M_HEADS, M_QK_DIM, EW), F32), pltpu.VMEM((8, 128), F32)],
        compiler_params=_cparams(("parallel", "arbitrary")),
        name="mlstm_prompt",
    )(z, z, z, z, z, gm_all, s0)
    return hm, sout[..., :M_V_DIM], sout[..., M_V_DIM], sout[:, :, 0, M_V_DIM + 1]


def _mlstm_step_body(zq_ref, zk_ref, zv_ref, zo_ref, misc_ref, gm_ref, c_ref, n_ref, m_ref,
                     hm_ref, cout_ref, nout_ref, mout_ref):
    misc = misc_ref[...]
    lf_all = _log_sigmoid(misc)
    eye = lax.broadcasted_iota(I32, (M_QK_DIM, M_QK_DIM), 0) == lax.broadcasted_iota(I32, (M_QK_DIM, M_QK_DIM), 1)
    for h in range(M_HEADS):
        q = zq_ref[:, h * M_QK_DIM:(h + 1) * M_QK_DIM] * (M_QK_DIM ** -0.5)
        k = zk_ref[:, h * M_QK_DIM:(h + 1) * M_QK_DIM]
        sl = slice(h * M_V_DIM, (h + 1) * M_V_DIM)
        v = zv_ref[:, sl]
        li = misc[:, L_MI + h:L_MI + h + 1]
        lf = lf_all[:, L_MF + h:L_MF + h + 1]
        s0 = c_ref[h]
        n0 = n_ref[h:h + 1, :]
        m0 = m_ref[:, h:h + 1]
        m = jnp.maximum(lf + m0, li)
        w = jnp.sum(q * k, axis=1, keepdims=True) * jnp.exp(li - m)
        c_inter = jnp.exp(lf + m0 - m)
        q8 = jnp.broadcast_to(q, (8, M_QK_DIM)).astype(BF16)
        qs = _dot(q8, s0.astype(BF16))[0:1, :]
        num = c_inter * qs + w * v
        den = c_inter * jnp.sum(q * n0, axis=1, keepdims=True) + w
        hh = num / jnp.maximum(jnp.abs(den), jnp.exp(-m))
        hm_ref[:, sl] = (_rms(hh, gm_ref[:, sl]) * jax.nn.sigmoid(zo_ref[:, sl])).astype(BF16)
        a = jnp.exp(li - m)
        kcol = jnp.sum(jnp.where(eye, jnp.broadcast_to(k, (M_QK_DIM, M_QK_DIM)), 0.0), axis=1, keepdims=True)
        cout_ref[h] = c_inter * s0 + (a * kcol) * v
        nout_ref[h:h + 1, :] = c_inter * n0 + a * k
        mout_ref[:, h:h + 1] = m


def mlstm_step(z, gm_all, state_c, state_n, state_m, layer):
    B = z.shape[0]
    z3 = z.reshape(B, 1, NZ)
    m4 = state_m.reshape(state_m.shape[0], B, 1, M_HEADS)
    zspec = lambda w, c0: pl.BlockSpec((None, 1, w), lambda b: (b, 0, c0 // w))
    hm, cout, nout, mout = pl.pallas_call(
        _mlstm_step_body,
        grid=(B,),
        in_specs=[zspec(M_QK, C_MQ), zspec(M_QK, C_MK), zspec(M_V, C_MV), zspec(M_V, C_MO), zspec(MISC_W, C_MISC),
                  pl.BlockSpec((None, 1, M_V), lambda b: (layer, 0, 0)),
                  pl.BlockSpec((None, None, M_HEADS, M_QK_DIM, M_V_DIM), lambda b: (layer, b, 0, 0, 0)),
                  pl.BlockSpec((None, None, M_HEADS, M_QK_DIM), lambda b: (layer, b, 0, 0)),
                  pl.BlockSpec((None, None, 1, M_HEADS), lambda b: (layer, b, 0, 0))],
        out_specs=[pl.BlockSpec((None, 1, M_V), lambda b: (b, 0, 0)),
                   pl.BlockSpec((None, M_HEADS, M_QK_DIM, M_V_DIM), lambda b: (b, 0, 0, 0)),
                   pl.BlockSpec((None, M_HEADS, M_QK_DIM), lambda b: (b, 0, 0)),
                   pl.BlockSpec((None, 1, M_HEADS), lambda b: (b, 0, 0))],
        out_shape=[jax.ShapeDtypeStruct((B, 1, M_V), BF16),
                   jax.ShapeDtypeStruct((B, M_HEADS, M_QK_DIM, M_V_DIM), F32),
                   jax.ShapeDtypeStruct((B, M_HEADS, M_QK_DIM), F32),
                   jax.ShapeDtypeStruct((B, 1, M_HEADS), F32)],
        compiler_params=_cparams(("parallel",)),
        name="mlstm_step",
    )(z3, z3, z3, z3, z3, gm_all, state_c, state_n, m4)
    return hm.reshape(B, M_V), cout, nout, mout.reshape(B, M_HEADS)


_INT_MIN = np.int32(-2 ** 31)
_KEY_NEG_INF = np.int32(np.array(0xFF800000, np.uint32).view(np.int32) ^ np.int32(0x7FFFFFFF))


def _sort_key(x):
    bits = pltpu.bitcast(x, I32)
    return jnp.where(bits < 0, bits ^ np.int32(0x7FFFFFFF), bits)


def _dsa_prompt_body(q_ref, iq_ref, miscq_ref, k_ref, v_ref, misck_ref, o_ref, key_ref, *, topk):
    i = pl.program_id(1)
    QB = q_ref.shape[0]
    KT = key_ref.shape[-1]
    nt = i + 1
    qpos = i * QB + lax.broadcasted_iota(I32, (QB, KT), 0)
    kloc = lax.broadcasted_iota(I32, (QB, KT), 1)
    miscq = miscq_ref[...]

    def score_tile(j, carry):
        off = pl.multiple_of(j * KT, KT)
        ik = misck_ref[pl.ds(off, KT), :][:, :IDX_DIM].astype(BF16)
        sc = jnp.zeros((QB, KT), F32)
        for h in range(IDX_HEADS):
            d = _dot_nt(iq_ref[:, h * IDX_DIM:(h + 1) * IDX_DIM], ik)
            sc = sc + jnp.maximum(d, 0.0) * miscq[:, L_IW + h:L_IW + h + 1]
        sc = jnp.where(j * KT + kloc <= qpos, sc, -jnp.inf)
        key_ref[j] = _sort_key(sc)
        return carry

    lax.fori_loop(0, nt, score_tile, 0)

    ones = jnp.ones((KT, 128), BF16)

    def count_ge(cand_s):
        cand2 = jnp.concatenate([cand_s] * (KT // 128), axis=1)

        def body(j, cnt):
            ge = jnp.where(key_ref[j] >= cand2, 1.0, 0.0).astype(BF16)
            return cnt + _dot(ge, ones)

        return lax.fori_loop(0, nt, body, jnp.zeros((QB, 128), F32))

    def bit_step(b, theta_u):
        bit = lax.shift_left(np.int32(1), np.int32(31) - b)
        cand_u = theta_u | bit
        cnt = count_ge(cand_u ^ _INT_MIN)
        return jnp.where(cnt >= float(topk), cand_u, theta_u)

    theta_u = lax.fori_loop(0, 32, bit_step, jnp.zeros((QB, 128), I32))
    theta_s = jnp.maximum(theta_u ^ _INT_MIN, _KEY_NEG_INF + np.int32(1))
    theta2 = jnp.concatenate([theta_s] * (KT // 128), axis=1)

    for h in range(N_HEADS):
        sl = slice(h * HEAD_DIM, (h + 1) * HEAD_DIM)
        qh = q_ref[:, sl]

        def att_tile(j, carry):
            m_old, l_old, acc = carry
            off = pl.multiple_of(j * KT, KT)
            s = _dot_nt(qh, k_ref[pl.ds(off, KT), sl])
            sel = key_ref[j] >= theta2
            m_new = jnp.maximum(m_old, jnp.max(jnp.where(sel, s, NEG_BIG), axis=1, keepdims=True))
            p = jnp.where(sel, jnp.exp(s - m_new), 0.0)
            alpha = jnp.exp(m_old - m_new)
            l_new = alpha * l_old + jnp.sum(p, axis=1, keepdims=True)
            acc = alpha * acc + _dot(p.astype(BF16), v_ref[pl.ds(off, KT), sl])
            return m_new, l_new, acc

        init = (jnp.full((QB, 1), NEG_BIG, F32), jnp.zeros((QB, 1), F32), jnp.zeros((QB, HEAD_DIM), F32))
        _, l_fin, acc = lax.fori_loop(0, nt, att_tile, init)
        o_ref[:, sl] = (acc / l_fin).astype(BF16)


def dsa_prompt(qb, iqb, miscr, kb, vb, *, B, S, QB):
    topk = min(TOPK_MAX, S // 4)
    nq = S // QB
    return pl.pallas_call(
        functools.partial(_dsa_prompt_body, topk=topk),
        grid=(B, nq),
        in_specs=[pl.BlockSpec((QB, A_W), lambda b, i: (b * nq + i, 0)),
                  pl.BlockSpec((QB, IDX_QW), lambda b, i: (b * nq + i, 0)),
                  pl.BlockSpec((QB, MISC_W), lambda b, i: (b * nq + i, 0)),
                  pl.BlockSpec((S, A_W), lambda b, i: (b, 0)),
                  pl.BlockSpec((S, A_W), lambda b, i: (b, 0)),
                  pl.BlockSpec((S, MISC_W), lambda b, i: (b, 0))],
        out_specs=pl.BlockSpec((QB, A_W), lambda b, i: (b * nq + i, 0)),
        out_shape=jax.ShapeDtypeStruct((B * S, A_W), BF16),
        scratch_shapes=[pltpu.VMEM((nq, QB, QB), I32)],
        compiler_params=_cparams(("parallel", "arbitrary")),
        name="dsa_prompt",
    )(qb, iqb, miscr, kb, vb, miscr)


PAGES_PER_STEP = 8


def _dsa_scores_body(pt_ref, iq_ref, coef_ref, *rest):
    page_refs, o_ref = rest[:-1], rest[-1]
    iq = iq_ref[...]
    coef = coef_ref[...]
    for r, kref in enumerate(page_refs):
        d = _dot_nt(iq, kref[...].astype(BF16))
        o_ref[r:r + 1, :] = jnp.sum(jnp.maximum(d, 0.0) * coef, axis=0, keepdims=True)


def dsa_sample_scores(iq3, coef3, cache_kidx, page_table, layer):
    B, n_pages = page_table.shape
    G = PAGES_PER_STEP

    def page_spec(r):
        return pl.BlockSpec((None, None, PAGE_SIZE, IDX_DIM), lambda b, j, pt: (layer, pt[b, j * G + r], 0, 0))

    return pl.pallas_call(
        _dsa_scores_body,
        grid_spec=pltpu.PrefetchScalarGridSpec(
            num_scalar_prefetch=1,
            grid=(B, n_pages // G),
            in_specs=[pl.BlockSpec((None, IDX_HEADS, IDX_DIM), lambda b, j, pt: (b, 0, 0)),
                      pl.BlockSpec((None, IDX_HEADS, 1), lambda b, j, pt: (b, 0, 0))]
                     + [page_spec(r) for r in range(G)],
            out_specs=pl.BlockSpec((None, G, PAGE_SIZE), lambda b, j, pt: (b, j, 0)),
        ),
        out_shape=jax.ShapeDtypeStruct((B, n_pages, PAGE_SIZE), F32),
        compiler_params=_cparams(("parallel", "arbitrary")),
        name="dsa_sample_scores",
    )(page_table, iq3, coef3, *([cache_kidx] * G))


def _dsa_sample_body(pt_ref, sc_ref, iq_ref, coef_ref, ikn_ref, q_ref, kn_ref, vn_ref, kp_ref, vp_ref,
                     o_ref, sel_ref, qbd_ref, m_ref, l_ref, acc_ref, newsel_ref, *, topk):
    p = pl.program_id(1)
    n_pages = pl.num_programs(1)
    blk = lax.broadcasted_iota(I32, (N_HEADS, A_W), 1) // HEAD_DIM == lax.broadcasted_iota(I32, (N_HEADS, A_W), 0)

    @pl.when(p == 0)
    def _():
        dn = jnp.sum(iq_ref[...].astype(F32) * ikn_ref[...].astype(F32), axis=1, keepdims=True)
        s_new = jnp.sum(jnp.maximum(dn, 0.0) * coef_ref[...], axis=0, keepdims=True)
        key = _sort_key(sc_ref[...])
        key_new = _sort_key(s_new)

        def bit_step(b, theta_u):
            bit = lax.shift_left(np.int32(1), np.int32(31) - b)
            cand_u = theta_u | bit
            cand_s = cand_u ^ _INT_MIN
            cnt = jnp.sum(jnp.sum(jnp.where(key >= cand_s, 1.0, 0.0), axis=1, keepdims=True), axis=0, keepdims=True)
            cnt = cnt + jnp.where(key_new >= cand_s, 1.0, 0.0)
            return jnp.where(cnt >= float(topk), cand_u, theta_u)

        theta_s = lax.fori_loop(0, 32, bit_step, jnp.zeros((1, 1), I32)) ^ _INT_MIN
        sel_ref[...] = jnp.where(key >= theta_s, 1.0, 0.0)
        newsel_ref[...] = jnp.broadcast_to(jnp.where(key_new >= theta_s, 1.0, 0.0), newsel_ref.shape)
        qbd_ref[...] = jnp.where(blk, jnp.broadcast_to(q_ref[...].astype(F32), (N_HEADS, A_W)), 0.0).astype(BF16)
        m_ref[...] = jnp.full(m_ref.shape, NEG_BIG, F32)
        l_ref[...] = jnp.zeros(l_ref.shape, F32)
        acc_ref[...] = jnp.zeros(acc_ref.shape, F32)

    sel = sel_ref[pl.ds(p, 1), :] > 0.5
    s = _dot_nt(qbd_ref[...], kp_ref[...].astype(BF16))
    m_old = m_ref[...]
    m_new = jnp.maximum(m_old, jnp.max(jnp.where(sel, s, NEG_BIG), axis=1, keepdims=True))
    pr = jnp.where(sel, jnp.exp(s - m_new), 0.0)
    alpha = jnp.exp(m_old - m_new)
    l_ref[...] = alpha * l_ref[...] + jnp.sum(pr, axis=1, keepdims=True)
    acc_ref[...] = alpha * acc_ref[...] + _dot(pr.astype(BF16), vp_ref[...].astype(BF16))
    m_ref[...] = m_new

    @pl.when(p == n_pages - 1)
    def _():
        seln = newsel_ref[:, 0:1] > 0.5
        sn = jnp.sum(qbd_ref[...].astype(F32) * kn_ref[...].astype(F32), axis=1, keepdims=True)
        m_old = m_ref[...]
        m_new = jnp.maximum(m_old, jnp.where(seln, sn, NEG_BIG))
        pn = jnp.where(seln, jnp.exp(sn - m_new), 0.0)
        alpha = jnp.exp(m_old - m_new)
        l_fin = alpha * l_ref[...] + pn
        acc = alpha * acc_ref[...] + pn.astype(BF16).astype(F32) * vn_ref[...].astype(F32)
        out = jnp.where(blk, acc / l_fin, 0.0)
        o_ref[...] = jnp.sum(out, axis=0, keepdims=True).astype(BF16)


def dsa_sample(scores, iq3, coef3, ikb3, qb3, kb3, vb3, cache_k4, cache_v4, page_table, layer):
    B, n_pages = page_table.shape
    topk = min(TOPK_MAX, (n_pages * PAGE_SIZE + 1) // 4)
    row = lambda w: pl.BlockSpec((None, 1, w), lambda b, p, pt: (b, 0, 0))
    page = pl.BlockSpec((None, None, PAGE_SIZE, A_W), lambda b, p, pt: (layer, pt[b, p], 0, 0))
    return pl.pallas_call(
        functools.partial(_dsa_sample_body, topk=topk),
        grid_spec=pltpu.PrefetchScalarGridSpec(
            num_scalar_prefetch=1,
            grid=(B, n_pages),
            in_specs=[pl.BlockSpec((None, n_pages, PAGE_SIZE), lambda b, p, pt: (b, 0, 0)),
                      pl.BlockSpec((None, IDX_HEADS, IDX_DIM), lambda b, p, pt: (b, 0, 0)),
                      pl.BlockSpec((None, IDX_HEADS, 1), lambda b, p, pt: (b, 0, 0)),
                      row(IDX_DIM), row(A_W), row(A_W), row(A_W), page, page],
            out_specs=row(A_W),
            scratch_shapes=[pltpu.VMEM((n_pages, PAGE_SIZE), F32), pltpu.VMEM((N_HEADS, A_W), BF16),
                            pltpu.VMEM((N_HEADS, 1), F32), pltpu.VMEM((N_HEADS, 1), F32),
                            pltpu.VMEM((N_HEADS, A_W), F32), pltpu.VMEM((N_HEADS, 128), F32)],
        ),
        out_shape=jax.ShapeDtypeStruct((B, 1, A_W), BF16),
        compiler_params=_cparams(("parallel", "arbitrary")),
        name="dsa_sample",
    )(page_table, scores, iq3, coef3, ikb3, qb3, kb3, vb3, cache_k4, cache_v4)


def _merge_body(x_ref, hm_ref, att_ref, ga_ref, gb_ref, wa_ref, wb_ref, wo_ref, o_ref):
    br_a = _dot(hm_ref[...], wa_ref[...])
    br_b = _dot(att_ref[...], wb_ref[...])
    t = jax.nn.sigmoid(ga_ref[...]) * br_a + jax.nn.sigmoid(gb_ref[...]) * br_b
    o_ref[...] = x_ref[...] + _dot(t.astype(BF16), wo_ref[...])


def merge(x, hm, att, z, wa_all, wb_all, wo_all, layer, *, tm):
    M, D = x.shape
    wspec = lambda k: pl.BlockSpec((None, k, D), lambda i: (layer, 0, 0))
    return pl.pallas_call(
        _merge_body,
        grid=(M // tm,),
        in_specs=[pl.BlockSpec((tm, D), lambda i: (i, 0)),
                  pl.BlockSpec((tm, M_V), lambda i: (i, 0)),
                  pl.BlockSpec((tm, A_W), lambda i: (i, 0)),
                  pl.BlockSpec((tm, D), lambda i: (i, C_GA // D)),
                  pl.BlockSpec((tm, D), lambda i: (i, C_GB // D)),
                  wspec(M_V), wspec(A_W), wspec(D)],
        out_specs=pl.BlockSpec((tm, D), lambda i: (i, 0)),
        out_shape=jax.ShapeDtypeStruct((M, D), F32),
        compiler_params=_cparams(("parallel",)),
        name="merge",
    )(x, hm, att, z, z, wa_all, wb_all, wo_all)


def relayout_w_in(w_in, b_in):
    def cols(a):
        mq, mk, mv, mo = a[..., 0:512], a[..., 512:1024], a[..., 1024:2048], a[..., 2048:3072]
        mi, mf = a[..., 3072:3076], a[..., 3076:3080]
        aq, ak, av = a[..., 3080:4104], a[..., 4104:5128], a[..., 5128:6152]
        iq, ik, iw = a[..., 6152:7176], a[..., 7176:7240], a[..., 7240:7256]
        ga, gb = a[..., 7256:9304], a[..., 9304:11352]
        pad = jnp.zeros(a.shape[:-1] + (NZ - C_MISC - L_MF - M_HEADS,), a.dtype)
        return jnp.concatenate([ga, gb, mq, mk, mv, mo, aq, ak, av, iq, ik, iw, mi, mf, pad], axis=-1)

    return cols(w_in).astype(BF16), cols(b_in)[:, None, :]


def _layer(x, layer, P, tabs, *, tm, tf, tn, tq, tab_blocks, mixers):
    x = ffn(x, P["g_f1"], P["w_f1_gu"], P["w_f1_dn"], layer, tm=tm, tf=tf)
    z = inproj(x, P["g_mix"], P["wr"], P["br"], layer, tm=tm, tn=tn)
    qb, kf, kb, vb, iqb, miscr = qkpost(z, P["g_q"], P["g_k"], P["g_ik"], tabs, layer, tm=tq, tab_blocks=tab_blocks)
    hm, att, state = mixers(z, qb, kb, vb, iqb, miscr)
    x = merge(x, hm, att, z, P["w_a"], P["w_b"], P["w_o"], layer, tm=tq)
    x = ffn(x, P["g_f2"], P["w_f2_gu"], P["w_f2_dn"], layer, tm=tm, tf=tf)
    rows = (kf, z[:, C_AV:C_AV + A_W], miscr[:, :IDX_DIM]) + state
    return x, rows


def kernel(x_prompt, x_sample, cache_k, cache_v, cache_kidx, state_C, state_n, state_m, page_table, norm_ffn1, w_ffn1_gu, w_ffn1_down, norm_mix, w_in, b_in, q_norm, k_norm, idx_k_norm, mlstm_norm, w_branch_a, w_branch_b, w_out, norm_ffn2, w_ffn2_gu, w_ffn2_down):
    B, S, D = x_prompt.shape
    BS = x_sample.shape[0]
    depth = w_in.shape[0]
    n_pool = cache_k.shape[1]
    past = page_table.shape[1] * PAGE_SIZE

    wr, br = relayout_w_in(w_in, b_in)
    pad128 = lambda g: jnp.pad(g, ((0, 0), (0, 128 - g.shape[-1])))[:, None, :]
    P = dict(g_f1=norm_ffn1[:, None, :], w_f1_gu=w_ffn1_gu, w_f1_dn=w_ffn1_down,
             g_mix=norm_mix[:, None, :], wr=wr, br=br,
             g_q=q_norm[:, None, :], g_k=k_norm[:, None, :], g_ik=pad128(idx_k_norm),
             g_m=mlstm_norm.reshape(depth, 1, M_V),
             w_a=w_branch_a.astype(BF16), w_b=w_branch_b.astype(BF16), w_o=w_out.astype(BF16),
             g_f2=norm_ffn2[:, None, :], w_f2_gu=w_ffn2_gu, w_f2_dn=w_ffn2_down)
    tabs_p = rope_tables(jnp.arange(S))
    tabs_s = rope_tables(jnp.full((BS,), past, jnp.int32))
    cache_k4 = cache_k.reshape(depth, n_pool, PAGE_SIZE, A_W)
    cache_v4 = cache_v.reshape(depth, n_pool, PAGE_SIZE, A_W)

    xp = x_prompt.reshape(B * S, D)
    xs = x_sample.reshape(BS, D)
    rows_p, rows_s = [], []
    for layer in range(depth):
        def mix_prompt(z, qb, kb, vb, iqb, miscr):
            hm, C, n, m = mlstm_prompt(z, P["g_m"], layer, B=B, S=S, T=256)
            att = dsa_prompt(qb, iqb, miscr, kb, vb, B=B, S=S, QB=256)
            return hm, att, (C, n, m)

        def mix_sample(z, qb, kb, vb, iqb, miscr):
            hm, C, n, m = mlstm_step(z, P["g_m"], state_C, state_n, state_m, layer)
            iq3 = iqb.reshape(BS, IDX_HEADS, IDX_DIM)
            coef3 = miscr[:, L_IW:L_IW + IDX_HEADS].reshape(BS, IDX_HEADS, 1)
            ikb3 = miscr[:, :IDX_DIM].astype(BF16).reshape(BS, 1, IDX_DIM)
            scores = dsa_sample_scores(iq3, coef3, cache_kidx, page_table, layer)
            att = dsa_sample(scores, iq3, coef3, ikb3, qb.reshape(BS, 1, A_W), kb.reshape(BS, 1, A_W),
                             vb.reshape(BS, 1, A_W), cache_k4, cache_v4, page_table, layer)
            return hm, att.reshape(BS, A_W), (C, n, m)

        xp, rp = _layer(xp, layer, P, tabs_p, tm=1024, tf=256, tn=1280, tq=512, tab_blocks=S // 512,
                        mixers=mix_prompt)
        xs, rs = _layer(xs, layer, P, tabs_s, tm=BS, tf=512, tn=1280, tq=BS, tab_blocks=1, mixers=mix_sample)
        rows_p.append(rp)
        rows_s.append(rs)

    def stack(rows, idx, shape):
        return jnp.stack([r[idx] for r in rows]).reshape((depth,) + shape)

    return (xp.reshape(B, S, D), xs.reshape(BS, 1, D),
            stack(rows_p, 0, (B, S, N_HEADS, HEAD_DIM)), stack(rows_p, 1, (B, S, N_HEADS, HEAD_DIM)),
            stack(rows_p, 2, (B, S, IDX_DIM)),
            stack(rows_p, 3, (B, M_HEADS, M_QK_DIM, M_V_DIM)), stack(rows_p, 4, (B, M_HEADS, M_QK_DIM)),
            stack(rows_p, 5, (B, M_HEADS)),
            stack(rows_s, 0, (BS, 1, N_HEADS, HEAD_DIM)), stack(rows_s, 1, (BS, 1, N_HEADS, HEAD_DIM)),
            stack(rows_s, 2, (BS, 1, IDX_DIM)),
            stack(rows_s, 3, (BS, M_HEADS, M_QK_DIM, M_V_DIM)), stack(rows_s, 4, (BS, M_HEADS, M_QK_DIM)),
            stack(rows_s, 5, (BS, M_HEADS)))
```

```python
import functools

import jax
import jax.numpy as jnp
import numpy as np
from jax import lax
from jax.experimental import pallas as pl
from jax.experimental.pallas import tpu as pltpu

F32 = jnp.float32
BF16 = jnp.bfloat16
I32 = jnp.int32

EPS = 1e-6
ROPE_THETA = 500000.0
PAGE_SIZE = 128
TOPK_MAX = 256

M_HEADS = 4
M_QK_DIM = 128
M_V_DIM = 256
N_HEADS = 8
HEAD_DIM = 128
IDX_HEADS = 16
IDX_DIM = 64
D_MODEL = 2048

M_QK = M_HEADS * M_QK_DIM
M_V = M_HEADS * M_V_DIM
A_W = N_HEADS * HEAD_DIM
IDX_QW = IDX_HEADS * IDX_DIM

C_GA = 0
C_GB = C_GA + D_MODEL
C_MQ = C_GB + D_MODEL
C_MK = C_MQ + M_QK
C_MV = C_MK + M_QK
C_MO = C_MV + M_V
C_AQ = C_MO + M_V
C_AK = C_AQ + A_W
C_AV = C_AK + A_W
C_IQ = C_AV + A_W
C_MISC = C_IQ + IDX_QW
L_IK = 0
L_IW = IDX_DIM
L_MI = L_IW + IDX_HEADS
L_MF = L_MI + M_HEADS
MISC_W = 128
NZ = 11520

VMEM_LIMIT = 58 * 1024 * 1024

NEG_BIG = -1e30
KEY_TILE = 256


def _cparams(sem, vmem=VMEM_LIMIT):
    return pltpu.CompilerParams(dimension_semantics=sem, vmem_limit_bytes=vmem)


def _rms(x, g):
    return x * lax.rsqrt(jnp.mean(x * x, axis=-1, keepdims=True) + EPS) * g


def _dot(a, b):
    return jnp.dot(a, b, preferred_element_type=F32)


def _dot_nt(a, b):
    return lax.dot_general(a, b, (((1,), (1,)), ((), ())), preferred_element_type=F32)


def _log_sigmoid(x):
    return jnp.minimum(x, 0.0) - jnp.log1p(jnp.exp(-jnp.abs(x)))


def _ffn_body(x_ref, g_ref, wg_ref, wu_ref, wd_ref, o_ref, h_ref):
    @pl.when(pl.program_id(1) == 0)
    def _():
        x = x_ref[...]
        h_ref[...] = _rms(x, g_ref[...]).astype(BF16)
        o_ref[...] = x

    h = h_ref[...]
    a = _dot(h, wg_ref[...].astype(BF16))
    u = _dot(h, wu_ref[...].astype(BF16))
    p = (0.5 * a) * jax.nn.sigmoid(a) * u
    o_ref[...] += _dot(p.astype(BF16), wd_ref[...].astype(BF16))


def ffn(x, g_all, wgu_all, wdn_all, layer, *, tm, tf):
    M, D = x.shape
    F = wdn_all.shape[1]
    nf = F // tf
    return pl.pallas_call(
        _ffn_body,
        grid=(M // tm, nf),
        in_specs=[
            pl.BlockSpec((tm, D), lambda i, j: (i, 0)),
            pl.BlockSpec((None, 1, D), lambda i, j: (layer, 0, 0)),
            pl.BlockSpec((None, D, tf), lambda i, j: (layer, 0, j)),
            pl.BlockSpec((None, D, tf), lambda i, j: (layer, 0, j + nf)),
            pl.BlockSpec((None, tf, D), lambda i, j: (layer, j, 0)),
        ],
        out_specs=pl.BlockSpec((tm, D), lambda i, j: (i, 0)),
        out_shape=jax.ShapeDtypeStruct((M, D), F32),
        scratch_shapes=[pltpu.VMEM((tm, D), BF16)],
        compiler_params=_cparams(("parallel", "arbitrary")),
        name="ffn",
    )(x, g_all, wgu_all, wgu_all, wdn_all)


def _inproj_body(x_ref, g_ref, w_ref, b_ref, z_ref, h_ref):
    @pl.when(pl.program_id(1) == 0)
    def _():
        h_ref[...] = _rms(x_ref[...], g_ref[...]).astype(BF16)

    z_ref[...] = _dot(h_ref[...], w_ref[...]) + b_ref[...]


def inproj(x, g_all, wr_all, br_all, layer, *, tm, tn):
    M, D = x.shape
    return pl.pallas_call(
        _inproj_body,
        grid=(M // tm, NZ // tn),
        in_specs=[
            pl.BlockSpec((tm, D), lambda i, j: (i, 0)),
            pl.BlockSpec((None, 1, D), lambda i, j: (layer, 0, 0)),
            pl.BlockSpec((None, D, tn), lambda i, j: (layer, 0, j)),
            pl.BlockSpec((None, 1, tn), lambda i, j: (layer, 0, j)),
        ],
        out_specs=pl.BlockSpec((tm, tn), lambda i, j: (i, j)),
        out_shape=jax.ShapeDtypeStruct((M, NZ), F32),
        scratch_shapes=[pltpu.VMEM((tm, D), BF16)],
        compiler_params=_cparams(("parallel", "arbitrary")),
        name="inproj",
    )(x, g_all, wr_all, br_all)


def _rope(x, c, sa, sb, shift):
    n = x.shape[-1]
    return x * c + pltpu.roll(x, n - shift, 1) * sa + pltpu.roll(x, shift, 1) * sb


def _qkpost_body(aq_ref, ak_ref, av_ref, iq_ref, misc_ref, gq_ref, gk_ref, gik_ref,
                 c128_ref, sa128_ref, sb128_ref, c64_ref, sa64_ref, sb64_ref,
                 q_ref, kf_ref, kb_ref, v_ref, iqo_ref, miscr_ref, *misct_ref, transposed):
    c128, sa128, sb128 = c128_ref[...], sa128_ref[...], sb128_ref[...]
    c64, sa64, sb64 = c64_ref[...], sa64_ref[...], sb64_ref[...]
    gq, gk = gq_ref[...], gk_ref[...]
    half128 = HEAD_DIM // 8
    half64 = IDX_DIM // 8
    tm = aq_ref.shape[0]
    for h in range(N_HEADS):
        sl = slice(h * HEAD_DIM, (h + 1) * HEAD_DIM)
        q = _rope(_rms(aq_ref[:, sl], gq), c128, sa128, sb128, half128) * (HEAD_DIM ** -0.5)
        k = _rope(_rms(ak_ref[:, sl], gk), c128, sa128, sb128, half128)
        kf_ref[:, sl] = k
        kb_ref[:, sl] = k.astype(BF16)
        if transposed:
            q_ref[sl, :] = q.T.astype(BF16)
            for t in range(tm // KEY_TILE):
                v_ref[t, sl, :] = av_ref[t * KEY_TILE:(t + 1) * KEY_TILE, sl].T.astype(BF16)
        else:
            q_ref[:, sl] = q.astype(BF16)
            v_ref[:, sl] = av_ref[:, sl].astype(BF16)
    for h in range(IDX_QW // 128):
        sl = slice(h * 128, (h + 1) * 128)
        iq = _rope(iq_ref[:, sl], c64, sa64, sb64, half64)
        if transposed:
            iqo_ref[sl, :] = iq.T.astype(BF16)
        else:
            iqo_ref[:, sl] = iq.astype(BF16)
    misc = misc_ref[...]
    lane = lax.broadcasted_iota(I32, misc.shape, 1)
    is_ik = lane < IDX_DIM
    ikv = jnp.where(is_ik, misc, 0.0)
    ms = jnp.sum(ikv * ikv, axis=-1, keepdims=True) * (1.0 / IDX_DIM)
    ikn = ikv * lax.rsqrt(ms + EPS) * gik_ref[...]
    ikr = _rope(ikn, c64, sa64, sb64, half64)
    is_iw = (lane >= L_IW) & (lane < L_IW + IDX_HEADS)
    iw_scale = (IDX_HEADS ** -0.5) * (IDX_DIM ** -0.5)
    miscr = jnp.where(is_ik, ikr, jnp.where(is_iw, misc * iw_scale, misc))
    miscr_ref[...] = miscr
    if transposed:
        misct_ref[0][...] = miscr.T


def qkpost(z, gq_all, gk_all, gik_all, tabs, layer, *, tm, tab_blocks, transposed):
    M = z.shape[0]
    zspec = lambda w, c: pl.BlockSpec((tm, w), lambda i: (i, c // w))
    tspec = pl.BlockSpec((tm, 128), lambda i: (i % tab_blocks, 0))
    gspec = pl.BlockSpec((None, 1, 128), lambda i: (layer, 0, 0))
    row = lambda w: pl.BlockSpec((tm, w), lambda i: (i, 0))
    col = lambda w: pl.BlockSpec((w, tm), lambda i: (0, i))
    sds = jax.ShapeDtypeStruct
    if transposed:
        nt = tm // KEY_TILE
        out_specs = [col(A_W), row(A_W), row(A_W), pl.BlockSpec((nt, A_W, KEY_TILE), lambda i: (i, 0, 0)),
                     col(IDX_QW), row(MISC_W), col(MISC_W)]
        out_shape = [sds((A_W, M), BF16), sds((M, A_W), F32), sds((M, A_W), BF16),
                     sds((M // KEY_TILE, A_W, KEY_TILE), BF16), sds((IDX_QW, M), BF16), sds((M, MISC_W), F32),
                     sds((MISC_W, M), F32)]
    else:
        out_specs = [row(A_W), row(A_W), row(A_W), row(A_W), row(IDX_QW), row(MISC_W)]
        out_shape = [sds((M, A_W), BF16), sds((M, A_W), F32), sds((M, A_W), BF16), sds((M, A_W), BF16),
                     sds((M, IDX_QW), BF16), sds((M, MISC_W), F32)]
    return pl.pallas_call(
        functools.partial(_qkpost_body, transposed=transposed),
        grid=(M // tm,),
        in_specs=[zspec(A_W, C_AQ), zspec(A_W, C_AK), zspec(A_W, C_AV), zspec(IDX_QW, C_IQ),
                  zspec(MISC_W, C_MISC), gspec, gspec, gspec] + [tspec] * 6,
        out_specs=out_specs,
        out_shape=out_shape,
        compiler_params=_cparams(("parallel",)),
        name="qkpost",
    )(z, z, z, z, z, gq_all, gk_all, gik_all, *tabs)


def rope_tables(pos):
    pos = pos.astype(F32)[:, None]

    def tables(width):
        rd = width // 4
        half = rd // 2
        inv = ROPE_THETA ** (-jnp.arange(half, dtype=F32) * (2.0 / rd))
        lane = np.arange(128) % width
        ang = pos * inv[None, :][:, lane % half]
        cos, sin = jnp.cos(ang), jnp.sin(ang)
        lo = jnp.asarray(lane < half)[None, :]
        hi = jnp.asarray((lane >= half) & (lane < rd))[None, :]
        c = jnp.where(lo | hi, cos, 1.0)
        sa = jnp.where(lo, -sin, 0.0)
        sb = jnp.where(hi, sin, 0.0)
        return c, sa, sb

    return tables(HEAD_DIM) + tables(IDX_DIM)


def _split3(x):
    hi = x.astype(BF16)
    r1 = x - hi.astype(F32)
    mid = r1.astype(BF16)
    lo = (r1 - mid.astype(F32)).astype(BF16)
    return hi, mid, lo


def _mlstm_body(zq_ref, zk_ref, zv_ref, zo_ref, misc_ref, gm_ref, s0_ref, hm_ref, sout_ref, s_ref, m_ref):
    c = pl.program_id(1)
    T = zq_ref.shape[0]
    EW = s_ref.shape[-1]

    @pl.when(c == 0)
    def _():
        s_ref[...] = s0_ref[...]
        for h in range(M_HEADS):
            m_ref[h:h + 1, :] = jnp.broadcast_to(s0_ref[h, 0:1, M_V_DIM + 1:M_V_DIM + 2], (1, 128))

    misc = misc_ref[...]
    misc_t = misc.T
    gates_t = misc_t[L_MI:L_MI + 8, :]
    lf_rows = _log_sigmoid(gates_t)
    lf_cols = _log_sigmoid(misc)
    ti = lax.broadcasted_iota(I32, (T, T), 0)
    si = lax.broadcasted_iota(I32, (T, T), 1)
    causal = si <= ti
    ltri = jnp.where(causal, 1.0, 0.0).astype(BF16)
    utri = jnp.where(ti <= si, 1.0, 0.0).astype(BF16)
    b_cols = sum(_dot(ltri, p) for p in _split3(lf_cols))
    b_rows = sum(_dot(p, utri) for p in _split3(lf_rows))
    ones_col = jnp.where(lax.broadcasted_iota(I32, (T, EW - M_V_DIM), 1) == 0, 1.0, 0.0).astype(BF16)

    for h in range(M_HEADS):
        q = (zq_ref[:, h * M_QK_DIM:(h + 1) * M_QK_DIM] * (M_QK_DIM ** -0.5)).astype(BF16)
        k = zk_ref[:, h * M_QK_DIM:(h + 1) * M_QK_DIM]
        ve = jnp.concatenate([zv_ref[:, h * M_V_DIM:(h + 1) * M_V_DIM].astype(BF16), ones_col], axis=1)
        bcol = b_cols[:, L_MF + h:L_MF + h + 1]
        brow = b_rows[M_HEADS + h:M_HEADS + h + 1, :]
        r = gates_t[h:h + 1, :] - brow
        m0 = m_ref[h:h + 1, 0:1]
        d = jnp.where(causal, bcol + r, -jnp.inf)
        inter = bcol + m0
        m = jnp.maximum(inter, jnp.max(d, axis=1, keepdims=True))
        w = (_dot_nt(q, k.astype(BF16)) * jnp.exp(d - m)).astype(BF16)
        c_inter = jnp.exp(inter - m)
        s0 = s_ref[h]
        nd = c_inter * _dot(q, s0.astype(BF16)) + _dot(w, ve)
        hh = nd[:, :M_V_DIM] / jnp.maximum(jnp.abs(nd[:, M_V_DIM:M_V_DIM + 1]), jnp.exp(-m))
        sl = slice(h * M_V_DIM, (h + 1) * M_V_DIM)
        hm_ref[:, sl] = (_rms(hh, gm_ref[:, sl]) * jax.nn.sigmoid(zo_ref[:, sl])).astype(BF16)

        b_end = brow[:, T - 1:T]
        g = b_end + r
        m_new = jnp.maximum(b_end + m0, jnp.max(g, axis=1, keepdims=True))
        a = jnp.exp(g - m_new)
        c_old = jnp.exp(b_end + m0 - m_new)
        s_ref[h] = c_old * s0 + _dot((k.T * a).astype(BF16), ve)
        m_ref[h:h + 1, :] = jnp.broadcast_to(m_new, (1, 128))

    @pl.when(c == pl.num_programs(1) - 1)
    def _():
        ri = lax.broadcasted_iota(I32, s_ref.shape[1:], 0)
        ci = lax.broadcasted_iota(I32, s_ref.shape[1:], 1)
        for h in range(M_HEADS):
            sout_ref[h] = jnp.where((ri == 0) & (ci == M_V_DIM + 1), m_ref[h:h + 1, 0:1], s_ref[h])


def mlstm_prompt(z, gm_all, layer, *, B, S, T):
    EW = M_V_DIM + 128
    nc = S // T
    s0 = jnp.zeros((M_HEADS, M_QK_DIM, EW), F32)
    zspec = lambda w, c0: pl.BlockSpec((T, w), lambda b, c: (b * nc + c, c0 // w))
    hm, sout = pl.pallas_call(
        _mlstm_body,
        grid=(B, nc),
        in_specs=[zspec(M_QK, C_MQ), zspec(M_QK, C_MK), zspec(M_V, C_MV), zspec(M_V, C_MO), zspec(MISC_W, C_MISC),
                  pl.BlockSpec((None, 1, M_V), lambda b, c: (layer, 0, 0)),
                  pl.BlockSpec((M_HEADS, M_QK_DIM, EW), lambda b, c: (0, 0, 0))],
        out_specs=[pl.BlockSpec((T, M_V), lambda b, c: (b * nc + c, 0)),
                   pl.BlockSpec((None, M_HEADS, M_QK_DIM, EW), lambda b, c: (b, 0, 0, 0))],
        out_shape=[jax.ShapeDtypeStruct((B * S, M_V), BF16),
                   jax.ShapeDtypeStruct((B, M_HEADS, M_QK_DIM, EW), F32)],
        scratch_shapes=[pltpu.VMEM((M_HEADS, M_QK_DIM, EW), F32), pltpu.VMEM((8, 128), F32)],
        compiler_params=_cparams(("parallel", "arbitrary")),
        name="mlstm_prompt",
    )(z, z, z, z, z, gm_all, s0)
    return hm, sout[..., :M_V_DIM], sout[..., M_V_DIM], sout[:, :, 0, M_V_DIM + 1]


def _mlstm_step_body(zq_ref, zk_ref, zv_ref, zo_ref, misc_ref, gm_ref, c_ref, n_ref, m_ref,
                     hm_ref, cout_ref, nout_ref, mout_ref):
    misc = misc_ref[...]
    lf_all = _log_sigmoid(misc)
    eye = lax.broadcasted_iota(I32, (M_QK_DIM, M_QK_DIM), 0) == lax.broadcasted_iota(I32, (M_QK_DIM, M_QK_DIM), 1)
    for h in range(M_HEADS):
        q = zq_ref[:, h * M_QK_DIM:(h + 1) * M_QK_DIM] * (M_QK_DIM ** -0.5)
        k = zk_ref[:, h * M_QK_DIM:(h + 1) * M_QK_DIM]
        sl = slice(h * M_V_DIM, (h + 1) * M_V_DIM)
        v = zv_ref[:, sl]
        li = misc[:, L_MI + h:L_MI + h + 1]
        lf = lf_all[:, L_MF + h:L_MF + h + 1]
        s0 = c_ref[h]
        n0 = n_ref[h:h + 1, :]
        m0 = m_ref[:, h:h + 1]
        m = jnp.maximum(lf + m0, li)
        w = jnp.sum(q * k, axis=1, keepdims=True) * jnp.exp(li - m)
        c_inter = jnp.exp(lf + m0 - m)
        q8 = jnp.broadcast_to(q, (8, M_QK_DIM)).astype(BF16)
        qs = _dot(q8, s0.astype(BF16))[0:1, :]
        num = c_inter * qs + w * v
        den = c_inter * jnp.sum(q * n0, axis=1, keepdims=True) + w
        hh = num / jnp.maximum(jnp.abs(den), jnp.exp(-m))
        hm_ref[:, sl] = (_rms(hh, gm_ref[:, sl]) * jax.nn.sigmoid(zo_ref[:, sl])).astype(BF16)
        a = jnp.exp(li - m)
        kcol = jnp.sum(jnp.where(eye, jnp.broadcast_to(k, (M_QK_DIM, M_QK_DIM)), 0.0), axis=1, keepdims=True)
        cout_ref[h] = c_inter * s0 + (a * kcol) * v
        nout_ref[h:h + 1, :] = c_inter * n0 + a * k
        mout_ref[:, h:h + 1] = m


def mlstm_step(z, gm_all, state_c, state_n, state_m, layer):
    B = z.shape[0]
    z3 = z.reshape(B, 1, NZ)
    m4 = state_m.reshape(state_m.shape[0], B, 1, M_HEADS)
    zspec = lambda w, c0: pl.BlockSpec((None, 1, w), lambda b: (b, 0, c0 // w))
    hm, cout, nout, mout = pl.pallas_call(
        _mlstm_step_body,
        grid=(B,),
        in_specs=[zspec(M_QK, C_MQ), zspec(M_QK, C_MK), zspec(M_V, C_MV), zspec(M_V, C_MO), zspec(MISC_W, C_MISC),
                  pl.BlockSpec((None, 1, M_V), lambda b: (layer, 0, 0)),
                  pl.BlockSpec((None, None, M_HEADS, M_QK_DIM, M_V_DIM), lambda b: (layer, b, 0, 0, 0)),
                  pl.BlockSpec((None, None, M_HEADS, M_QK_DIM), lambda b: (layer, b, 0, 0)),
                  pl.BlockSpec((None, None, 1, M_HEADS), lambda b: (layer, b, 0, 0))],
        out_specs=[pl.BlockSpec((None, 1, M_V), lambda b: (b, 0, 0)),
                   pl.BlockSpec((None, M_HEADS, M_QK_DIM, M_V_DIM), lambda b: (b, 0, 0, 0)),
                   pl.BlockSpec((None, M_HEADS, M_QK_DIM), lambda b: (b, 0, 0)),
                   pl.BlockSpec((None, 1, M_HEADS), lambda b: (b, 0, 0))],
        out_shape=[jax.ShapeDtypeStruct((B, 1, M_V), BF16),
                   jax.ShapeDtypeStruct((B, M_HEADS, M_QK_DIM, M_V_DIM), F32),
                   jax.ShapeDtypeStruct((B, M_HEADS, M_QK_DIM), F32),
                   jax.ShapeDtypeStruct((B, 1, M_HEADS), F32)],
        compiler_params=_cparams(("parallel",)),
        name="mlstm_step",
    )(z3, z3, z3, z3, z3, gm_all, state_c, state_n, m4)
    return hm.reshape(B, M_V), cout, nout, mout.reshape(B, M_HEADS)


_INT_MIN = np.int32(-2 ** 31)
_KEY_NEG_INF = np.int32(np.array(0xFF800000, np.uint32).view(np.int32) ^ np.int32(0x7FFFFFFF))


def _sort_key(x):
    bits = pltpu.bitcast(x, I32)
    return jnp.where(bits < 0, bits ^ np.int32(0x7FFFFFFF), bits)


def _topk_select(count, topk, nbits, shape):
    def bit_step(b, theta_u):
        cand_u = theta_u | lax.shift_left(np.int32(1), np.int32(31) - b)
        cand_s = cand_u ^ _INT_MIN
        cnt = count(lambda key, idx: jnp.where(key >= cand_s, 1.0, 0.0))
        return jnp.where(cnt >= float(topk), cand_u, theta_u)

    theta = lax.fori_loop(0, 32, bit_step, jnp.zeros(shape, I32)) ^ _INT_MIN
    room = float(topk) - count(lambda key, idx: jnp.where(key > theta, 1.0, 0.0))

    def tie_step(b, jlim):
        cand = jlim | lax.shift_left(np.int32(1), np.int32(nbits - 1) - b)
        f = count(lambda key, idx: jnp.where(key == theta, jnp.where(idx < cand, 1.0, 0.0), 0.0))
        return jnp.where(f <= room, cand, jlim)

    return theta, room, tie_step


def _dsa_prompt_body(qt_ref, iqt_ref, misct_ref, k_ref, vt_ref, misck_ref, o_ref, key_ref, jlim_ref, *, topk, nbits):
    i = pl.program_id(1)
    KT, QB = key_ref.shape[1:]
    nt = i + 1
    kloc = lax.broadcasted_iota(I32, (KT, QB), 0)
    qpos = i * QB + lax.broadcasted_iota(I32, (KT, QB), 1)
    coef = misct_ref[L_IW:L_IW + IDX_HEADS, :]

    def score_tile(j, carry):
        off = pl.multiple_of(j * KT, KT)
        ik = misck_ref[pl.ds(off, KT), :][:, :IDX_DIM].astype(BF16)
        sc = jnp.zeros((KT, QB), F32)
        for h in range(IDX_HEADS):
            d = _dot(ik, iqt_ref[h * IDX_DIM:(h + 1) * IDX_DIM, :])
            sc = sc + jnp.maximum(d, 0.0) * coef[h:h + 1, :]
        sc = jnp.where(j * KT + kloc <= qpos, sc, -jnp.inf)
        key_ref[j] = _sort_key(sc)
        return carry

    lax.fori_loop(0, nt, score_tile, 0)

    def count(f):
        def body(j, cnt):
            return cnt + jnp.sum(f(key_ref[j], j * KT + kloc), axis=0, keepdims=True)

        return lax.fori_loop(0, nt, body, jnp.zeros((1, QB), F32))

    theta, room, tie_step = _topk_select(count, topk, nbits, (1, QB))
    n_tie = count(lambda key, idx: jnp.where(key == theta, 1.0, 0.0))
    jlim_ref[...] = jnp.full((1, QB), 2 ** nbits, I32)
    crowded = jnp.max(jnp.where((n_tie > room) & (theta > _KEY_NEG_INF), 1.0, 0.0))

    @pl.when(crowded > 0.0)
    def _():
        jlim_ref[...] = lax.fori_loop(0, nbits, tie_step, jnp.zeros((1, QB), I32))

    jlim = jlim_ref[...]

    def finalize(j, carry):
        key = key_ref[j]
        sel = jnp.where(key > theta, 1, jnp.where(key == theta, jnp.where(j * KT + kloc < jlim, 1, 0), 0))
        key_ref[j] = jnp.where(key > _KEY_NEG_INF, sel, 0)
        return carry

    lax.fori_loop(0, nt, finalize, 0)

    for h in range(N_HEADS):
        sl = slice(h * HEAD_DIM, (h + 1) * HEAD_DIM)
        qh = qt_ref[sl, :]

        def att_tile(j, carry):
            m_old, l_old, acc = carry
            off = pl.multiple_of(j * KT, KT)
            s = _dot(k_ref[pl.ds(off, KT), sl], qh)
            sel = key_ref[j] > 0
            m_new = jnp.maximum(m_old, jnp.max(jnp.where(sel, s, NEG_BIG), axis=0, keepdims=True))
            p = jnp.where(sel, jnp.exp(s - m_new), 0.0)
            alpha = jnp.exp(m_old - m_new)
            l_new = alpha * l_old + jnp.sum(p, axis=0, keepdims=True)
            acc = alpha * acc + _dot(vt_ref[j, sl, :], p.astype(BF16))
            return m_new, l_new, acc

        init = (jnp.full((1, QB), NEG_BIG, F32), jnp.zeros((1, QB), F32), jnp.zeros((HEAD_DIM, QB), F32))
        _, l_fin, acc = lax.fori_loop(0, nt, att_tile, init)
        o_ref[:, sl] = (acc / l_fin).T.astype(BF16)


def dsa_prompt(qt, iqt, misct, miscr, kb, vt, *, B, S):
    QB = KEY_TILE
    topk = min(TOPK_MAX, S // 4)
    nq = S // QB
    col = lambda w: pl.BlockSpec((w, QB), lambda b, i: (0, b * nq + i))
    return pl.pallas_call(
        functools.partial(_dsa_prompt_body, topk=topk, nbits=S.bit_length()),
        grid=(B, nq),
        in_specs=[col(A_W), col(IDX_QW), col(MISC_W),
                  pl.BlockSpec((S, A_W), lambda b, i: (b, 0)),
                  pl.BlockSpec((nq, A_W, KEY_TILE), lambda b, i: (b, 0, 0)),
                  pl.BlockSpec((S, MISC_W), lambda b, i: (b, 0))],
        out_specs=pl.BlockSpec((QB, A_W), lambda b, i: (b * nq + i, 0)),
        out_shape=jax.ShapeDtypeStruct((B * S, A_W), BF16),
        scratch_shapes=[pltpu.VMEM((nq, KEY_TILE, QB), I32), pltpu.VMEM((1, QB), I32)],
        compiler_params=_cparams(("parallel", "arbitrary")),
        name="dsa_prompt",
    )(qt, iqt, misct, kb, vt, miscr)


PAGES_PER_STEP = 8
WORD_BITS = 16
WORDS_PER_PAGE = PAGE_SIZE // WORD_BITS
META_COUNT = 0
META_NEW = 1


def _dsa_select_body(pt_ref, iq_ref, coef_ref, ikn_ref, *rest, topk, nbits):
    G = PAGES_PER_STEP
    page_refs = rest[:G]
    words_ref, meta_ref, sc_ref = rest[G:]
    j = pl.program_id(1)
    iq = iq_ref[...]
    coef = coef_ref[...]
    for r, kref in enumerate(page_refs):
        d = _dot_nt(iq, kref[...].astype(BF16))
        sc_ref[pl.ds(j * G + r, 1), :] = jnp.sum(jnp.maximum(d, 0.0) * coef, axis=0, keepdims=True)

    @pl.when(j == pl.num_programs(1) - 1)
    def _():
        n_pages = sc_ref.shape[0]
        dn = jnp.sum(iq.astype(F32) * ikn_ref[...].astype(F32), axis=1, keepdims=True)
        key_new = _sort_key(jnp.sum(jnp.maximum(dn, 0.0) * coef, axis=0, keepdims=True))
        key = _sort_key(sc_ref[...])
        idx = (lax.broadcasted_iota(I32, key.shape, 0) * PAGE_SIZE + lax.broadcasted_iota(I32, key.shape, 1))
        idx_new = jnp.full((1, 1), n_pages * PAGE_SIZE, I32)

        def count(f):
            past = jnp.sum(jnp.sum(f(key, idx), axis=1, keepdims=True), axis=0, keepdims=True)
            return past + f(key_new, idx_new)

        theta, room, tie_step = _topk_select(count, topk, nbits, (1, 1))
        jlim = lax.fori_loop(0, nbits, tie_step, jnp.zeros((1, 1), I32))
        pick = lambda k, ix: jnp.where(k > theta, 1.0, jnp.where(k == theta, jnp.where(ix < jlim, 1.0, 0.0), 0.0))
        slot = lax.broadcasted_iota(I32, (WORDS_PER_PAGE, PAGE_SIZE), 1)
        word = lax.broadcasted_iota(I32, (WORDS_PER_PAGE, PAGE_SIZE), 0)
        in_word = lax.shift_right_logical(slot, np.int32(WORD_BITS.bit_length() - 1)) == word
        bit_value = jnp.where(in_word, lax.shift_left(np.int32(1), slot & np.int32(WORD_BITS - 1)), 0)
        picked = pick(key, idx)
        words = _dot_nt(bit_value.astype(F32).astype(BF16), picked.astype(BF16))
        words_ref[...] = words.astype(I32)
        n_picked = jnp.sum(jnp.sum(picked, axis=1, keepdims=True), axis=0, keepdims=True)
        lane = lax.broadcasted_iota(I32, meta_ref.shape, 1)
        meta = jnp.where(lane == META_COUNT, n_picked, jnp.where(lane == META_NEW, pick(key_new, idx_new), 0.0))
        meta_ref[...] = meta.astype(I32)


def dsa_sample_select(iq3, coef3, ikb3, cache_kidx, page_table, layer):
    B, n_pages = page_table.shape
    G = PAGES_PER_STEP
    n_keys = n_pages * PAGE_SIZE + 1
    topk = min(TOPK_MAX, n_keys // 4)

    def page_spec(r):
        return pl.BlockSpec((None, None, PAGE_SIZE, IDX_DIM), lambda b, j, pt: (layer, pt[b, j * G + r], 0, 0))

    return pl.pallas_call(
        functools.partial(_dsa_select_body, topk=topk, nbits=n_keys.bit_length()),
        grid_spec=pltpu.PrefetchScalarGridSpec(
            num_scalar_prefetch=1,
            grid=(B, n_pages // G),
            in_specs=[pl.BlockSpec((None, IDX_HEADS, IDX_DIM), lambda b, j, pt: (b, 0, 0)),
                      pl.BlockSpec((None, IDX_HEADS, 1), lambda b, j, pt: (b, 0, 0)),
                      pl.BlockSpec((None, 1, IDX_DIM), lambda b, j, pt: (b, 0, 0))]
                     + [page_spec(r) for r in range(G)],
            out_specs=[pl.BlockSpec((None, WORDS_PER_PAGE, n_pages), lambda b, j, pt: (b, 0, 0)),
                       pl.BlockSpec((None, 1, 128), lambda b, j, pt: (b, 0, 0))],
            scratch_shapes=[pltpu.VMEM((n_pages, PAGE_SIZE), F32)],
        ),
        out_shape=[jax.ShapeDtypeStruct((B, WORDS_PER_PAGE, n_pages), I32),
                   jax.ShapeDtypeStruct((B, 1, 128), I32)],
        compiler_params=_cparams(("parallel", "arbitrary")),
        name="dsa_sample_select",
    )(page_table, iq3, coef3, ikb3, *([cache_kidx] * G))


def _row_copies(ck_ref, cv_ref, kbuf, vbuf, sem, layer, phys, slot, row):
    dst = pl.ds(pl.multiple_of(row * N_HEADS, N_HEADS), N_HEADS)
    return (pltpu.make_async_copy(ck_ref.at[layer, phys, slot], kbuf.at[dst, :], sem.at[0]),
            pltpu.make_async_copy(cv_ref.at[layer, phys, slot], vbuf.at[dst, :], sem.at[1]))


def _dsa_gather_body(pt_ref, words_ref, meta_ref, q_ref, kn_ref, vn_ref, ck_ref, cv_ref, o_ref,
                     kbuf, vbuf, cnt_ref, sem, *, layer, cap):
    b = pl.program_id(0)
    n_words = words_ref.shape[-1] * WORDS_PER_PAGE
    want = jnp.minimum(meta_ref[0, META_COUNT], cap)
    kbuf[...] = jnp.zeros(kbuf.shape, F32)
    vbuf[...] = jnp.zeros(vbuf.shape, F32)
    cnt_ref[0] = 0

    def scan_word(carry):
        i, _ = carry
        p = lax.shift_right_logical(i, np.int32(WORDS_PER_PAGE.bit_length() - 1))
        w = i & np.int32(WORDS_PER_PAGE - 1)
        word = words_ref[w, p]

        @pl.when(word != 0)
        def _():
            phys = pt_ref[b, p]

            def bit_loop(bit, c):
                n = cnt_ref[0]

                @pl.when(((lax.shift_right_logical(word, bit) & 1) != 0) & (n < want))
                def _():
                    for cp in _row_copies(ck_ref, cv_ref, kbuf, vbuf, sem, layer, phys, w * WORD_BITS + bit, n):
                        cp.start()
                    cnt_ref[0] = n + 1

                return c

            lax.fori_loop(0, WORD_BITS, bit_loop, 0)

        return i + 1, cnt_ref[0]

    _, cnt = lax.while_loop(lambda c: (c[0] < n_words) & (c[1] < want), scan_word, (np.int32(0), np.int32(0)))

    def wait_loop(n, carry):
        for cp in _row_copies(ck_ref, cv_ref, kbuf, vbuf, sem, layer, 0, 0, n):
            cp.wait()
        return carry

    lax.fori_loop(0, cnt, wait_loop, 0)

    q8 = q_ref[...]
    valid = lax.broadcasted_iota(I32, (N_HEADS, cap), 1) < cnt
    seln = meta_ref[0, META_NEW] > 0
    sn = jnp.sum(q8.astype(F32) * kn_ref[...].astype(F32), axis=1, keepdims=True)
    vn = vn_ref[...].astype(F32)
    head = lax.broadcasted_iota(I32, (N_HEADS, HEAD_DIM), 0)
    out = jnp.zeros((N_HEADS, HEAD_DIM), F32)
    for h in range(N_HEADS):
        kh = kbuf[pl.ds(h, cap, stride=N_HEADS), :].astype(BF16)
        vh = vbuf[pl.ds(h, cap, stride=N_HEADS), :].astype(BF16)
        s = _dot_nt(q8, kh)
        m = jnp.maximum(jnp.max(jnp.where(valid, s, NEG_BIG), axis=1, keepdims=True), jnp.where(seln, sn, NEG_BIG))
        p = jnp.where(valid, jnp.exp(s - m), 0.0)
        pn = jnp.where(seln, jnp.exp(sn - m), 0.0)
        l = jnp.sum(p, axis=1, keepdims=True) + pn
        o = (_dot(p.astype(BF16), vh) + pn.astype(BF16).astype(F32) * vn) / l
        out = jnp.where(head == h, o, out)
    o_ref[...] = out.astype(BF16)


def dsa_sample_gather(words, meta, q8, kn8, vn8, cache_k, cache_v, page_table, layer):
    B, n_pages = page_table.shape
    cap = min(TOPK_MAX, (n_pages * PAGE_SIZE + 1) // 4)
    head_rows = pl.BlockSpec((None, N_HEADS, HEAD_DIM), lambda b, pt: (b, 0, 0))
    return pl.pallas_call(
        functools.partial(_dsa_gather_body, layer=layer, cap=cap),
        grid_spec=pltpu.PrefetchScalarGridSpec(
            num_scalar_prefetch=1,
            grid=(B,),
            in_specs=[pl.BlockSpec((None, WORDS_PER_PAGE, n_pages), lambda b, pt: (b, 0, 0), memory_space=pltpu.SMEM),
                      pl.BlockSpec((None, 1, 128), lambda b, pt: (b, 0, 0), memory_space=pltpu.SMEM),
                      head_rows, head_rows, head_rows,
                      pl.BlockSpec(memory_space=pl.ANY), pl.BlockSpec(memory_space=pl.ANY)],
            out_specs=head_rows,
            scratch_shapes=[pltpu.VMEM((cap * N_HEADS, HEAD_DIM), F32), pltpu.VMEM((cap * N_HEADS, HEAD_DIM), F32),
                            pltpu.SMEM((1,), I32), pltpu.SemaphoreType.DMA((2,))],
        ),
        out_shape=jax.ShapeDtypeStruct((B, N_HEADS, HEAD_DIM), BF16),
        compiler_params=_cparams(("arbitrary",)),
        name="dsa_sample_gather",
    )(page_table, words, meta, q8, kn8, vn8, cache_k, cache_v)


def _merge_body(x_ref, hm_ref, att_ref, ga_ref, gb_ref, wa_ref, wb_ref, wo_ref, o_ref):
    br_a = _dot(hm_ref[...], wa_ref[...])
    br_b = _dot(att_ref[...], wb_ref[...])
    t = jax.nn.sigmoid(ga_ref[...]) * br_a + jax.nn.sigmoid(gb_ref[...]) * br_b
    o_ref[...] = x_ref[...] + _dot(t.astype(BF16), wo_ref[...])


def merge(x, hm, att, z, wa_all, wb_all, wo_all, layer, *, tm):
    M, D = x.shape
    wspec = lambda k: pl.BlockSpec((None, k, D), lambda i: (layer, 0, 0))
    return pl.pallas_call(
        _merge_body,
        grid=(M // tm,),
        in_specs=[pl.BlockSpec((tm, D), lambda i: (i, 0)),
                  pl.BlockSpec((tm, M_V), lambda i: (i, 0)),
                  pl.BlockSpec((tm, A_W), lambda i: (i, 0)),
                  pl.BlockSpec((tm, D), lambda i: (i, C_GA // D)),
                  pl.BlockSpec((tm, D), lambda i: (i, C_GB // D)),
                  wspec(M_V), wspec(A_W), wspec(D)],
        out_specs=pl.BlockSpec((tm, D), lambda i: (i, 0)),
        out_shape=jax.ShapeDtypeStruct((M, D), F32),
        compiler_params=_cparams(("parallel",)),
        name="merge",
    )(x, hm, att, z, z, wa_all, wb_all, wo_all)


def relayout_w_in(w_in, b_in):
    def cols(a):
        mq, mk, mv, mo = a[..., 0:512], a[..., 512:1024], a[..., 1024:2048], a[..., 2048:3072]
        mi, mf = a[..., 3072:3076], a[..., 3076:3080]
        aq, ak, av = a[..., 3080:4104], a[..., 4104:5128], a[..., 5128:6152]
        iq, ik, iw = a[..., 6152:7176], a[..., 7176:7240], a[..., 7240:7256]
        ga, gb = a[..., 7256:9304], a[..., 9304:11352]
        pad = jnp.zeros(a.shape[:-1] + (NZ - C_MISC - L_MF - M_HEADS,), a.dtype)
        return jnp.concatenate([ga, gb, mq, mk, mv, mo, aq, ak, av, iq, ik, iw, mi, mf, pad], axis=-1)

    return cols(w_in).astype(BF16), cols(b_in)[:, None, :]


def _layer(x, layer, P, tabs, *, tm, tf, tn, tq, tab_blocks, transposed, mixers):
    x = ffn(x, P["g_f1"], P["w_f1_gu"], P["w_f1_dn"], layer, tm=tm, tf=tf)
    z = inproj(x, P["g_mix"], P["wr"], P["br"], layer, tm=tm, tn=tn)
    post = qkpost(z, P["g_q"], P["g_k"], P["g_ik"], tabs, layer, tm=tq, tab_blocks=tab_blocks, transposed=transposed)
    kf, miscr = post[1], post[5]
    hm, att, state = mixers(z, post)
    x = merge(x, hm, att, z, P["w_a"], P["w_b"], P["w_o"], layer, tm=tq)
    x = ffn(x, P["g_f2"], P["w_f2_gu"], P["w_f2_dn"], layer, tm=tm, tf=tf)
    rows = (kf, z[:, C_AV:C_AV + A_W], miscr[:, :IDX_DIM]) + state
    return x, rows


def kernel(x_prompt, x_sample, cache_k, cache_v, cache_kidx, state_C, state_n, state_m, page_table, norm_ffn1, w_ffn1_gu, w_ffn1_down, norm_mix, w_in, b_in, q_norm, k_norm, idx_k_norm, mlstm_norm, w_branch_a, w_branch_b, w_out, norm_ffn2, w_ffn2_gu, w_ffn2_down):
    B, S, D = x_prompt.shape
    BS = x_sample.shape[0]
    depth = w_in.shape[0]
    past = page_table.shape[1] * PAGE_SIZE

    wr, br = relayout_w_in(w_in, b_in)
    pad128 = lambda g: jnp.pad(g, ((0, 0), (0, 128 - g.shape[-1])))[:, None, :]
    P = dict(g_f1=norm_ffn1[:, None, :], w_f1_gu=w_ffn1_gu, w_f1_dn=w_ffn1_down,
             g_mix=norm_mix[:, None, :], wr=wr, br=br,
             g_q=q_norm[:, None, :], g_k=k_norm[:, None, :], g_ik=pad128(idx_k_norm),
             g_m=mlstm_norm.reshape(depth, 1, M_V),
             w_a=w_branch_a.astype(BF16), w_b=w_branch_b.astype(BF16), w_o=w_out.astype(BF16),
             g_f2=norm_ffn2[:, None, :], w_f2_gu=w_ffn2_gu, w_f2_dn=w_ffn2_down)
    tabs_p = rope_tables(jnp.arange(S))
    tabs_s = rope_tables(jnp.full((BS,), past, jnp.int32))

    xp = x_prompt.reshape(B * S, D)
    xs = x_sample.reshape(BS, D)
    rows_p, rows_s = [], []
    for layer in range(depth):
        def mix_prompt(z, post):
            qt, _, kb, vt, iqt, miscr, misct = post
            hm, C, n, m = mlstm_prompt(z, P["g_m"], layer, B=B, S=S, T=256)
            att = dsa_prompt(qt, iqt, misct, miscr, kb, vt, B=B, S=S)
            return hm, att, (C, n, m)

        def mix_sample(z, post):
            qb, _, kb, vb, iqb, miscr = post
            hm, C, n, m = mlstm_step(z, P["g_m"], state_C, state_n, state_m, layer)
            iq3 = iqb.reshape(BS, IDX_HEADS, IDX_DIM)
            coef3 = miscr[:, L_IW:L_IW + IDX_HEADS].reshape(BS, IDX_HEADS, 1)
            ikb3 = miscr[:, :IDX_DIM].astype(BF16).reshape(BS, 1, IDX_DIM)
            words, meta = dsa_sample_select(iq3, coef3, ikb3, cache_kidx, page_table, layer)
            heads = lambda a: a.reshape(BS, N_HEADS, HEAD_DIM)
            att = dsa_sample_gather(words, meta, heads(qb), heads(kb), heads(vb), cache_k, cache_v, page_table, layer)
            return hm, att.reshape(BS, A_W), (C, n, m)

        xp, rp = _layer(xp, layer, P, tabs_p, tm=1024, tf=256, tn=1280, tq=512, tab_blocks=S // 512,
                        transposed=True, mixers=mix_prompt)
        xs, rs = _layer(xs, layer, P, tabs_s, tm=BS, tf=512, tn=1280, tq=BS, tab_blocks=1,
                        transposed=False, mixers=mix_sample)
        rows_p.append(rp)
        rows_s.append(rs)

    def stack(rows, idx, shape):
        return jnp.stack([r[idx] for r in rows]).reshape((depth,) + shape)

    return (xp.reshape(B, S, D), xs.reshape(BS, 1, D),
            stack(rows_p, 0, (B, S, N_HEADS, HEAD_DIM)), stack(rows_p, 1, (B, S, N_HEADS, HEAD_DIM)),
            stack(rows_p, 2, (B, S, IDX_DIM)),
            stack(rows_p, 3, (B, M_HEADS, M_QK_DIM, M_V_DIM)), stack(rows_p, 4, (B, M_HEADS, M_QK_DIM)),
            stack(rows_p, 5, (B, M_HEADS)),
            stack(rows_s, 0, (BS, 1, N_HEADS, HEAD_DIM)), stack(rows_s, 1, (BS, 1, N_HEADS, HEAD_DIM)),
            stack(rows_s, 2, (BS, 1, IDX_DIM)),
            stack(rows_s, 3, (BS, M_HEADS, M_QK_DIM, M_V_DIM)), stack(rows_s, 4, (BS, M_HEADS, M_QK_DIM)),
            stack(rows_s, 5, (BS, M_HEADS)))
```

```python
import functools

import jax
import jax.numpy as jnp
import numpy as np
from jax import lax
from jax.experimental import pallas as pl
from jax.experimental.pallas import tpu as pltpu

F32 = jnp.float32
BF16 = jnp.bfloat16
I32 = jnp.int32

EPS = 1e-6
ROPE_THETA = 500000.0
PAGE_SIZE = 128
TOPK_MAX = 256

M_HEADS = 4
M_QK_DIM = 128
M_V_DIM = 256
N_HEADS = 8
HEAD_DIM = 128
IDX_HEADS = 16
IDX_DIM = 64
D_MODEL = 2048

M_QK = M_HEADS * M_QK_DIM
M_V = M_HEADS * M_V_DIM
A_W = N_HEADS * HEAD_DIM
IDX_QW = IDX_HEADS * IDX_DIM

C_GA = 0
C_GB = C_GA + D_MODEL
C_MQ = C_GB + D_MODEL
C_MK = C_MQ + M_QK
C_MV = C_MK + M_QK
C_MO = C_MV + M_V
C_AQ = C_MO + M_V
C_AK = C_AQ + A_W
C_AV = C_AK + A_W
C_IQ = C_AV + A_W
C_MISC = C_IQ + IDX_QW
L_IK = 0
L_IW = IDX_DIM
L_MI = L_IW + IDX_HEADS
L_MF = L_MI + M_HEADS
MISC_W = 128
NZ = 11520

VMEM_LIMIT = 58 * 1024 * 1024

NEG_BIG = -1e30
MAX_INIT = -1e20
KEY_TILE = 256


def _cparams(sem, vmem=VMEM_LIMIT):
    return pltpu.CompilerParams(dimension_semantics=sem, vmem_limit_bytes=vmem)


def _rms(x, g):
    return x * lax.rsqrt(jnp.mean(x * x, axis=-1, keepdims=True) + EPS) * g


def _dot(a, b):
    return jnp.dot(a, b, preferred_element_type=F32)


def _dot_nt(a, b):
    return lax.dot_general(a, b, (((1,), (1,)), ((), ())), preferred_element_type=F32)


def _log_sigmoid(x):
    return jnp.minimum(x, 0.0) - jnp.log1p(jnp.exp(-jnp.abs(x)))


def _ffn_body(x_ref, g_ref, wg_ref, wu_ref, wd_ref, o_ref, h_ref):
    @pl.when(pl.program_id(1) == 0)
    def _():
        x = x_ref[...]
        h_ref[...] = _rms(x, g_ref[...]).astype(BF16)
        o_ref[...] = x

    h = h_ref[...]
    a = _dot(h, wg_ref[...].astype(BF16))
    u = _dot(h, wu_ref[...].astype(BF16))
    p = (0.5 * a) * jax.nn.sigmoid(a) * u
    o_ref[...] += _dot(p.astype(BF16), wd_ref[...].astype(BF16))


def ffn(x, g_all, wgu_all, wdn_all, layer, *, tm, tf):
    M, D = x.shape
    F = wdn_all.shape[1]
    nf = F // tf
    return pl.pallas_call(
        _ffn_body,
        grid=(M // tm, nf),
        in_specs=[
            pl.BlockSpec((tm, D), lambda i, j: (i, 0)),
            pl.BlockSpec((None, 1, D), lambda i, j: (layer, 0, 0)),
            pl.BlockSpec((None, D, tf), lambda i, j: (layer, 0, j)),
            pl.BlockSpec((None, D, tf), lambda i, j: (layer, 0, j + nf)),
            pl.BlockSpec((None, tf, D), lambda i, j: (layer, j, 0)),
        ],
        out_specs=pl.BlockSpec((tm, D), lambda i, j: (i, 0)),
        out_shape=jax.ShapeDtypeStruct((M, D), F32),
        scratch_shapes=[pltpu.VMEM((tm, D), BF16)],
        compiler_params=_cparams(("parallel", "arbitrary")),
        name="ffn",
    )(x, g_all, wgu_all, wgu_all, wdn_all)


def _inproj_body(x_ref, g_ref, w_ref, b_ref, z_ref, h_ref):
    @pl.when(pl.program_id(1) == 0)
    def _():
        h_ref[...] = _rms(x_ref[...], g_ref[...]).astype(BF16)

    z_ref[...] = _dot(h_ref[...], w_ref[...]) + b_ref[...]


def inproj(x, g_all, wr_all, br_all, layer, *, tm, tn):
    M, D = x.shape
    return pl.pallas_call(
        _inproj_body,
        grid=(M // tm, NZ // tn),
        in_specs=[
            pl.BlockSpec((tm, D), lambda i, j: (i, 0)),
            pl.BlockSpec((None, 1, D), lambda i, j: (layer, 0, 0)),
            pl.BlockSpec((None, D, tn), lambda i, j: (layer, 0, j)),
            pl.BlockSpec((None, 1, tn), lambda i, j: (layer, 0, j)),
        ],
        out_specs=pl.BlockSpec((tm, tn), lambda i, j: (i, j)),
        out_shape=jax.ShapeDtypeStruct((M, NZ), F32),
        scratch_shapes=[pltpu.VMEM((tm, D), BF16)],
        compiler_params=_cparams(("parallel", "arbitrary")),
        name="inproj",
    )(x, g_all, wr_all, br_all)


def _rope(x, c, sa, sb, shift):
    n = x.shape[-1]
    return x * c + pltpu.roll(x, n - shift, 1) * sa + pltpu.roll(x, shift, 1) * sb


def _qkpost_body(aq_ref, ak_ref, av_ref, iq_ref, misc_ref, gq_ref, gk_ref, gik_ref,
                 c128_ref, sa128_ref, sb128_ref, c64_ref, sa64_ref, sb64_ref,
                 q_ref, kf_ref, kb_ref, vf_ref, v_ref, iqo_ref, miscr_ref, *misct_ref, transposed):
    c128, sa128, sb128 = c128_ref[...], sa128_ref[...], sb128_ref[...]
    c64, sa64, sb64 = c64_ref[...], sa64_ref[...], sb64_ref[...]
    gq, gk = gq_ref[...], gk_ref[...]
    half128 = HEAD_DIM // 8
    half64 = IDX_DIM // 8
    tm = aq_ref.shape[0]
    for h in range(N_HEADS):
        sl = slice(h * HEAD_DIM, (h + 1) * HEAD_DIM)
        q = _rope(_rms(aq_ref[:, sl], gq), c128, sa128, sb128, half128) * (HEAD_DIM ** -0.5)
        k = _rope(_rms(ak_ref[:, sl], gk), c128, sa128, sb128, half128)
        kf_ref[:, sl] = k
        kb_ref[:, sl] = k.astype(BF16)
        vf_ref[:, sl] = av_ref[:, sl]
        if transposed:
            q_ref[sl, :] = q.T.astype(BF16)
            for t in range(tm // KEY_TILE):
                v_ref[t, sl, :] = av_ref[t * KEY_TILE:(t + 1) * KEY_TILE, sl].T.astype(BF16)
        else:
            q_ref[:, sl] = q.astype(BF16)
            v_ref[:, sl] = av_ref[:, sl].astype(BF16)
    for h in range(IDX_QW // 128):
        sl = slice(h * 128, (h + 1) * 128)
        iq = _rope(iq_ref[:, sl], c64, sa64, sb64, half64)
        if transposed:
            iqo_ref[sl, :] = iq.T.astype(BF16)
        else:
            iqo_ref[:, sl] = iq.astype(BF16)
    misc = misc_ref[...]
    lane = lax.broadcasted_iota(I32, misc.shape, 1)
    is_ik = lane < IDX_DIM
    ikv = jnp.where(is_ik, misc, 0.0)
    ms = jnp.sum(ikv * ikv, axis=-1, keepdims=True) * (1.0 / IDX_DIM)
    ikn = ikv * lax.rsqrt(ms + EPS) * gik_ref[...]
    ikr = _rope(ikn, c64, sa64, sb64, half64)
    is_iw = (lane >= L_IW) & (lane < L_IW + IDX_HEADS)
    iw_scale = (IDX_HEADS ** -0.5) * (IDX_DIM ** -0.5)
    miscr = jnp.where(is_ik, ikr, jnp.where(is_iw, misc * iw_scale, misc))
    miscr_ref[...] = miscr
    if transposed:
        misct_ref[0][...] = miscr.T


def qkpost(z, gq_all, gk_all, gik_all, tabs, layer, *, tm, tab_blocks, transposed):
    M = z.shape[0]
    zspec = lambda w, c: pl.BlockSpec((tm, w), lambda i: (i, c // w))
    tspec = pl.BlockSpec((tm, 128), lambda i: (i % tab_blocks, 0))
    gspec = pl.BlockSpec((None, 1, 128), lambda i: (layer, 0, 0))
    row = lambda w: pl.BlockSpec((tm, w), lambda i: (i, 0))
    col = lambda w: pl.BlockSpec((w, tm), lambda i: (0, i))
    sds = jax.ShapeDtypeStruct
    if transposed:
        nt = tm // KEY_TILE
        out_specs = [col(A_W), row(A_W), row(A_W), row(A_W), pl.BlockSpec((nt, A_W, KEY_TILE), lambda i: (i, 0, 0)),
                     col(IDX_QW), row(MISC_W), col(MISC_W)]
        out_shape = [sds((A_W, M), BF16), sds((M, A_W), F32), sds((M, A_W), BF16), sds((M, A_W), F32),
                     sds((M // KEY_TILE, A_W, KEY_TILE), BF16), sds((IDX_QW, M), BF16), sds((M, MISC_W), F32),
                     sds((MISC_W, M), F32)]
    else:
        out_specs = [row(A_W), row(A_W), row(A_W), row(A_W), row(A_W), row(IDX_QW), row(MISC_W)]
        out_shape = [sds((M, A_W), BF16), sds((M, A_W), F32), sds((M, A_W), BF16), sds((M, A_W), F32),
                     sds((M, A_W), BF16), sds((M, IDX_QW), BF16), sds((M, MISC_W), F32)]
    return pl.pallas_call(
        functools.partial(_qkpost_body, transposed=transposed),
        grid=(M // tm,),
        in_specs=[zspec(A_W, C_AQ), zspec(A_W, C_AK), zspec(A_W, C_AV), zspec(IDX_QW, C_IQ),
                  zspec(MISC_W, C_MISC), gspec, gspec, gspec] + [tspec] * 6,
        out_specs=out_specs,
        out_shape=out_shape,
        compiler_params=_cparams(("parallel",)),
        name="qkpost",
    )(z, z, z, z, z, gq_all, gk_all, gik_all, *tabs)


def rope_tables(pos):
    pos = pos.astype(F32)[:, None]

    def tables(width):
        rd = width // 4
        half = rd // 2
        inv = ROPE_THETA ** (-jnp.arange(half, dtype=F32) * (2.0 / rd))
        lane = np.arange(128) % width
        ang = pos * inv[None, :][:, lane % half]
        cos, sin = jnp.cos(ang), jnp.sin(ang)
        lo = jnp.asarray(lane < half)[None, :]
        hi = jnp.asarray((lane >= half) & (lane < rd))[None, :]
        c = jnp.where(lo | hi, cos, 1.0)
        sa = jnp.where(lo, -sin, 0.0)
        sb = jnp.where(hi, sin, 0.0)
        return c, sa, sb

    return tables(HEAD_DIM) + tables(IDX_DIM)


def _split3(x):
    hi = x.astype(BF16)
    r1 = x - hi.astype(F32)
    mid = r1.astype(BF16)
    lo = (r1 - mid.astype(F32)).astype(BF16)
    return hi, mid, lo


def _mlstm_body(zq_ref, zk_ref, zv_ref, zo_ref, misc_ref, gm_ref, s0_ref, hm_ref, sout_ref, s_ref, m_ref):
    c = pl.program_id(1)
    T = zq_ref.shape[0]
    EW = s_ref.shape[-1]

    @pl.when(c == 0)
    def _():
        s_ref[...] = s0_ref[...]
        for h in range(M_HEADS):
            m_ref[h:h + 1, :] = jnp.broadcast_to(s0_ref[h, 0:1, M_V_DIM + 1:M_V_DIM + 2], (1, 128))

    misc = misc_ref[...]
    misc_t = misc.T
    gates_t = misc_t[L_MI:L_MI + 8, :]
    lf_rows = _log_sigmoid(gates_t)
    lf_cols = _log_sigmoid(misc)
    ti = lax.broadcasted_iota(I32, (T, T), 0)
    si = lax.broadcasted_iota(I32, (T, T), 1)
    causal = si <= ti
    ltri = jnp.where(causal, 1.0, 0.0).astype(BF16)
    utri = jnp.where(ti <= si, 1.0, 0.0).astype(BF16)
    b_cols = sum(_dot(ltri, p) for p in _split3(lf_cols))
    b_rows = sum(_dot(p, utri) for p in _split3(lf_rows))
    ones_col = jnp.where(lax.broadcasted_iota(I32, (T, EW - M_V_DIM), 1) == 0, 1.0, 0.0).astype(BF16)

    for h in range(M_HEADS):
        q = (zq_ref[:, h * M_QK_DIM:(h + 1) * M_QK_DIM] * (M_QK_DIM ** -0.5)).astype(BF16)
        k = zk_ref[:, h * M_QK_DIM:(h + 1) * M_QK_DIM]
        ve = jnp.concatenate([zv_ref[:, h * M_V_DIM:(h + 1) * M_V_DIM].astype(BF16), ones_col], axis=1)
        bcol = b_cols[:, L_MF + h:L_MF + h + 1]
        brow = b_rows[M_HEADS + h:M_HEADS + h + 1, :]
        r = gates_t[h:h + 1, :] - brow
        m0 = m_ref[h:h + 1, 0:1]
        d = jnp.where(causal, bcol + r, -jnp.inf)
        inter = bcol + m0
        m = jnp.maximum(inter, jnp.max(d, axis=1, keepdims=True))
        w = (_dot_nt(q, k.astype(BF16)) * jnp.exp(d - m)).astype(BF16)
        c_inter = jnp.exp(inter - m)
        s0 = s_ref[h]
        nd = c_inter * _dot(q, s0.astype(BF16)) + _dot(w, ve)
        hh = nd[:, :M_V_DIM] / jnp.maximum(jnp.abs(nd[:, M_V_DIM:M_V_DIM + 1]), jnp.exp(-m))
        sl = slice(h * M_V_DIM, (h + 1) * M_V_DIM)
        hm_ref[:, sl] = (_rms(hh, gm_ref[:, sl]) * jax.nn.sigmoid(zo_ref[:, sl])).astype(BF16)

        b_end = brow[:, T - 1:T]
        g = b_end + r
        m_new = jnp.maximum(b_end + m0, jnp.max(g, axis=1, keepdims=True))
        a = jnp.exp(g - m_new)
        c_old = jnp.exp(b_end + m0 - m_new)
        s_ref[h] = c_old * s0 + _dot((k.T * a).astype(BF16), ve)
        m_ref[h:h + 1, :] = jnp.broadcast_to(m_new, (1, 128))

    @pl.when(c == pl.num_programs(1) - 1)
    def _():
        ri = lax.broadcasted_iota(I32, s_ref.shape[1:], 0)
        ci = lax.broadcasted_iota(I32, s_ref.shape[1:], 1)
        for h in range(M_HEADS):
            sout_ref[h] = jnp.where((ri == 0) & (ci == M_V_DIM + 1), m_ref[h:h + 1, 0:1], s_ref[h])


def mlstm_prompt(z, gm_all, layer, *, B, S, T):
    EW = M_V_DIM + 128
    nc = S // T
    s0 = jnp.zeros((M_HEADS, M_QK_DIM, EW), F32)
    zspec = lambda w, c0: pl.BlockSpec((T, w), lambda b, c: (b * nc + c, c0 // w))
    hm, sout = pl.pallas_call(
        _mlstm_body,
        grid=(B, nc),
        in_specs=[zspec(M_QK, C_MQ), zspec(M_QK, C_MK), zspec(M_V, C_MV), zspec(M_V, C_MO), zspec(MISC_W, C_MISC),
                  pl.BlockSpec((None, 1, M_V), lambda b, c: (layer, 0, 0)),
                  pl.BlockSpec((M_HEADS, M_QK_DIM, EW), lambda b, c: (0, 0, 0))],
        out_specs=[pl.BlockSpec((T, M_V), lambda b, c: (b * nc + c, 0)),
                   pl.BlockSpec((None, M_HEADS, M_QK_DIM, EW), lambda b, c: (b, 0, 0, 0))],
        out_shape=[jax.ShapeDtypeStruct((B * S, M_V), BF16),
                   jax.ShapeDtypeStruct((B, M_HEADS, M_QK_DIM, EW), F32)],
        scratch_shapes=[pltpu.VMEM((M_HEADS, M_QK_DIM, EW), F32), pltpu.VMEM((8, 128), F32)],
        compiler_params=_cparams(("parallel", "arbitrary")),
        name="mlstm_prompt",
    )(z, z, z, z, z, gm_all, s0)
    return hm, sout[..., :M_V_DIM], sout[..., M_V_DIM], sout[:, :, 0, M_V_DIM + 1]


def _mlstm_step_body(zq_ref, zk_ref, zv_ref, zo_ref, misc_ref, gm_ref, c_ref, n_ref, m_ref,
                     hm_ref, cout_ref, nout_ref, mout_ref):
    misc = misc_ref[...]
    lf_all = _log_sigmoid(misc)
    eye = lax.broadcasted_iota(I32, (M_QK_DIM, M_QK_DIM), 0) == lax.broadcasted_iota(I32, (M_QK_DIM, M_QK_DIM), 1)
    for h in range(M_HEADS):
        q = zq_ref[:, h * M_QK_DIM:(h + 1) * M_QK_DIM] * (M_QK_DIM ** -0.5)
        k = zk_ref[:, h * M_QK_DIM:(h + 1) * M_QK_DIM]
        sl = slice(h * M_V_DIM, (h + 1) * M_V_DIM)
        v = zv_ref[:, sl]
        li = misc[:, L_MI + h:L_MI + h + 1]
        lf = lf_all[:, L_MF + h:L_MF + h + 1]
        s0 = c_ref[h]
        n0 = n_ref[h:h + 1, :]
        m0 = m_ref[:, h:h + 1]
        m = jnp.maximum(lf + m0, li)
        w = jnp.sum(q * k, axis=1, keepdims=True) * jnp.exp(li - m)
        c_inter = jnp.exp(lf + m0 - m)
        q8 = jnp.broadcast_to(q, (8, M_QK_DIM)).astype(BF16)
        qs = _dot(q8, s0.astype(BF16))[0:1, :]
        num = c_inter * qs + w * v
        den = c_inter * jnp.sum(q * n0, axis=1, keepdims=True) + w
        hh = num / jnp.maximum(jnp.abs(den), jnp.exp(-m))
        hm_ref[:, sl] = (_rms(hh, gm_ref[:, sl]) * jax.nn.sigmoid(zo_ref[:, sl])).astype(BF16)
        a = jnp.exp(li - m)
        kcol = jnp.sum(jnp.where(eye, jnp.broadcast_to(k, (M_QK_DIM, M_QK_DIM)), 0.0), axis=1, keepdims=True)
        cout_ref[h] = c_inter * s0 + (a * kcol) * v
        nout_ref[h:h + 1, :] = c_inter * n0 + a * k
        mout_ref[:, h:h + 1] = m


def mlstm_step(z, gm_all, state_c, state_n, state_m, layer):
    B = z.shape[0]
    z3 = z.reshape(B, 1, NZ)
    m4 = state_m.reshape(state_m.shape[0], B, 1, M_HEADS)
    zspec = lambda w, c0: pl.BlockSpec((None, 1, w), lambda b: (b, 0, c0 // w))
    hm, cout, nout, mout = pl.pallas_call(
        _mlstm_step_body,
        grid=(B,),
        in_specs=[zspec(M_QK, C_MQ), zspec(M_QK, C_MK), zspec(M_V, C_MV), zspec(M_V, C_MO), zspec(MISC_W, C_MISC),
                  pl.BlockSpec((None, 1, M_V), lambda b: (layer, 0, 0)),
                  pl.BlockSpec((None, None, M_HEADS, M_QK_DIM, M_V_DIM), lambda b: (layer, b, 0, 0, 0)),
                  pl.BlockSpec((None, None, M_HEADS, M_QK_DIM), lambda b: (layer, b, 0, 0)),
                  pl.BlockSpec((None, None, 1, M_HEADS), lambda b: (layer, b, 0, 0))],
        out_specs=[pl.BlockSpec((None, 1, M_V), lambda b: (b, 0, 0)),
                   pl.BlockSpec((None, M_HEADS, M_QK_DIM, M_V_DIM), lambda b: (b, 0, 0, 0)),
                   pl.BlockSpec((None, M_HEADS, M_QK_DIM), lambda b: (b, 0, 0)),
                   pl.BlockSpec((None, 1, M_HEADS), lambda b: (b, 0, 0))],
        out_shape=[jax.ShapeDtypeStruct((B, 1, M_V), BF16),
                   jax.ShapeDtypeStruct((B, M_HEADS, M_QK_DIM, M_V_DIM), F32),
                   jax.ShapeDtypeStruct((B, M_HEADS, M_QK_DIM), F32),
                   jax.ShapeDtypeStruct((B, 1, M_HEADS), F32)],
        compiler_params=_cparams(("parallel",)),
        name="mlstm_step",
    )(z3, z3, z3, z3, z3, gm_all, state_c, state_n, m4)
    return hm.reshape(B, M_V), cout, nout, mout.reshape(B, M_HEADS)


_INT_MIN = np.int32(-2 ** 31)
_KEY_NEG_INF = np.int32(np.array(0xFF800000, np.uint32).view(np.int32) ^ np.int32(0x7FFFFFFF))


def _sort_key(x):
    bits = pltpu.bitcast(x, I32)
    return jnp.where(bits < 0, bits ^ np.int32(0x7FFFFFFF), bits)


def _topk_select(count, topk, nbits, shape):
    def bit_step(b, theta_u):
        cand_u = theta_u | lax.shift_left(np.int32(1), np.int32(31) - b)
        cand_s = cand_u ^ _INT_MIN
        cnt = count(lambda key, idx: jnp.where(key >= cand_s, 1.0, 0.0))
        return jnp.where(cnt >= float(topk), cand_u, theta_u)

    theta = lax.fori_loop(0, 32, bit_step, jnp.zeros(shape, I32)) ^ _INT_MIN
    room = float(topk) - count(lambda key, idx: jnp.where(key > theta, 1.0, 0.0))

    def tie_step(b, jlim):
        cand = jlim | lax.shift_left(np.int32(1), np.int32(nbits - 1) - b)
        f = count(lambda key, idx: jnp.where(key == theta, jnp.where(idx < cand, 1.0, 0.0), 0.0))
        return jnp.where(f <= room, cand, jlim)

    return theta, room, tie_step


def _dsa_prompt_body(qt_ref, iqt_ref, misct_ref, k_ref, vt_ref, misck_ref, o_ref,
                     key_ref, jlim_ref, m_ref, l_ref, alpha_ref, acc_ref, bias_ref, s_ref, p_ref, *, topk, nbits):
    i = pl.program_id(1)
    KT, QB = key_ref.shape[1:]
    nt = i + 1
    kloc = lax.broadcasted_iota(I32, (KT, QB), 0)
    qpos = i * QB + lax.broadcasted_iota(I32, (KT, QB), 1)
    coef = misct_ref[L_IW:L_IW + IDX_HEADS, :]

    def score_tile(j, carry):
        off = pl.multiple_of(j * KT, KT)
        ik = misck_ref[pl.ds(off, KT), :][:, :IDX_DIM].astype(BF16)
        sc = jnp.zeros((KT, QB), F32)
        for h in range(IDX_HEADS):
            d = _dot(ik, iqt_ref[h * IDX_DIM:(h + 1) * IDX_DIM, :])
            sc = sc + jnp.maximum(d, 0.0) * coef[h:h + 1, :]
        sc = jnp.where(j * KT + kloc <= qpos, sc, -jnp.inf)
        key_ref[j] = _sort_key(sc)
        return carry

    lax.fori_loop(0, nt, score_tile, 0)

    def count(f):
        def body(j, cnt8):
            return cnt8 + jnp.sum(f(key_ref[j], j * KT + kloc).reshape(KT // 8, 8, QB), axis=0)

        return jnp.sum(lax.fori_loop(0, nt, body, jnp.zeros((8, QB), F32)), axis=0, keepdims=True)

    theta, room, tie_step = _topk_select(count, topk, nbits, (1, QB))
    n_tie = count(lambda key, idx: jnp.where(key == theta, 1.0, 0.0))
    jlim_ref[...] = jnp.full((1, QB), 2 ** nbits, I32)
    crowded = jnp.max(jnp.where((n_tie > room) & (theta > _KEY_NEG_INF), 1.0, 0.0))

    @pl.when(crowded > 0.0)
    def _():
        jlim_ref[...] = lax.fori_loop(0, nbits, tie_step, jnp.zeros((1, QB), I32))

    jlim = jlim_ref[...]

    def finalize(j, carry):
        key = key_ref[j]
        sel = jnp.where(key > theta, 1, jnp.where(key == theta, jnp.where(j * KT + kloc < jlim, 1, 0), 0))
        key_ref[j] = jnp.where(key > _KEY_NEG_INF, sel, 0)
        return carry

    lax.fori_loop(0, nt, finalize, 0)

    m_ref[...] = jnp.full(m_ref.shape, MAX_INIT, F32)
    l_ref[...] = jnp.zeros(l_ref.shape, F32)
    acc_ref[...] = jnp.zeros(acc_ref.shape, F32)

    def att_tile(j, carry):
        off = pl.multiple_of(j * KT, KT)
        bias_ref[...] = jnp.where(key_ref[j] > 0, 0.0, NEG_BIG)
        for h in range(N_HEADS):
            sl = slice(h * HEAD_DIM, (h + 1) * HEAD_DIM)
            s_ref[h] = _dot(k_ref[pl.ds(off, KT), sl], qt_ref[sl, :]) + bias_ref[...]
        for h in range(N_HEADS):
            m_old = m_ref[h:h + 1, :]
            m_new = jnp.maximum(m_old, jnp.max(s_ref[h], axis=0, keepdims=True))
            p = jnp.exp(s_ref[h] - m_new)
            alpha = jnp.exp(m_old - m_new)
            l_ref[h:h + 1, :] = alpha * l_ref[h:h + 1, :] + jnp.sum(p, axis=0, keepdims=True)
            p_ref[h] = p.astype(BF16)
            alpha_ref[h:h + 1, :] = alpha
            m_ref[h:h + 1, :] = m_new
        for h in range(N_HEADS):
            sl = slice(h * HEAD_DIM, (h + 1) * HEAD_DIM)
            acc_ref[sl, :] = alpha_ref[h:h + 1, :] * acc_ref[sl, :] + _dot(vt_ref[j, sl, :], p_ref[h])
        return carry

    lax.fori_loop(0, nt, att_tile, 0)
    for h in range(N_HEADS):
        sl = slice(h * HEAD_DIM, (h + 1) * HEAD_DIM)
        o_ref[:, sl] = (acc_ref[sl, :] / l_ref[h:h + 1, :]).T.astype(BF16)


def dsa_prompt(qt, iqt, misct, miscr, kb, vt, *, B, S):
    QB = KEY_TILE
    topk = min(TOPK_MAX, S // 4)
    nq = S // QB
    col = lambda w: pl.BlockSpec((w, QB), lambda b, i: (0, b * nq + i))
    return pl.pallas_call(
        functools.partial(_dsa_prompt_body, topk=topk, nbits=S.bit_length()),
        grid=(B, nq),
        in_specs=[col(A_W), col(IDX_QW), col(MISC_W),
                  pl.BlockSpec((S, A_W), lambda b, i: (b, 0)),
                  pl.BlockSpec((nq, A_W, KEY_TILE), lambda b, i: (b, 0, 0)),
                  pl.BlockSpec((S, MISC_W), lambda b, i: (b, 0))],
        out_specs=pl.BlockSpec((QB, A_W), lambda b, i: (b * nq + i, 0)),
        out_shape=jax.ShapeDtypeStruct((B * S, A_W), BF16),
        scratch_shapes=[pltpu.VMEM((nq, KEY_TILE, QB), I32), pltpu.VMEM((1, QB), I32),
                        pltpu.VMEM((N_HEADS, QB), F32), pltpu.VMEM((N_HEADS, QB), F32), pltpu.VMEM((N_HEADS, QB), F32),
                        pltpu.VMEM((A_W, QB), F32), pltpu.VMEM((KEY_TILE, QB), F32),
                        pltpu.VMEM((N_HEADS, KEY_TILE, QB), F32), pltpu.VMEM((N_HEADS, KEY_TILE, QB), BF16)],
        compiler_params=_cparams(("parallel", "arbitrary")),
        name="dsa_prompt",
    )(qt, iqt, misct, kb, vt, miscr)


PAGES_PER_STEP = 16
WORD_BITS = 32
HALF_BITS = WORD_BITS // 2
WORDS_PER_PAGE = PAGE_SIZE // WORD_BITS
CTZ_MULTIPLIER = np.int32(0x077CB531)
CTZ_TABLE = np.array([0, 1, 28, 2, 29, 14, 24, 3, 30, 22, 20, 15, 25, 17, 4, 8,
                      31, 27, 13, 23, 21, 19, 16, 7, 26, 12, 18, 6, 11, 5, 10, 9], np.int32)
META_COUNT = 0
META_NEW = 1


def _dsa_select_body(pt_ref, iq_ref, coef_ref, ikn_ref, *rest, topk, nbits):
    page_refs = rest[:-3]
    words_ref, meta_ref, sc_ref = rest[-3:]
    G = len(page_refs)
    j = pl.program_id(1)
    iq = iq_ref[...]
    coef = coef_ref[...]
    for r, kref in enumerate(page_refs):
        d = _dot_nt(iq, kref[...].astype(BF16))
        sc_ref[pl.ds(j * G + r, 1), :] = jnp.sum(jnp.maximum(d, 0.0) * coef, axis=0, keepdims=True)

    @pl.when(j == pl.num_programs(1) - 1)
    def _():
        n_pages = sc_ref.shape[0]
        dn = jnp.sum(iq.astype(F32) * ikn_ref[...].astype(F32), axis=1, keepdims=True)
        key_new = _sort_key(jnp.sum(jnp.maximum(dn, 0.0) * coef, axis=0, keepdims=True))
        key = _sort_key(sc_ref[...])
        idx = (lax.broadcasted_iota(I32, key.shape, 0) * PAGE_SIZE + lax.broadcasted_iota(I32, key.shape, 1))
        idx_new = jnp.full((1, 1), n_pages * PAGE_SIZE, I32)

        def count(f):
            past = jnp.sum(jnp.sum(f(key, idx), axis=1, keepdims=True), axis=0, keepdims=True)
            return past + f(key_new, idx_new)

        theta, room, tie_step = _topk_select(count, topk, nbits, (1, 1))
        jlim = lax.fori_loop(0, nbits, tie_step, jnp.zeros((1, 1), I32))
        pick = lambda k, ix: jnp.where(k > theta, 1.0, jnp.where(k == theta, jnp.where(ix < jlim, 1.0, 0.0), 0.0))
        slot = lax.broadcasted_iota(I32, (2 * WORDS_PER_PAGE, PAGE_SIZE), 1)
        row = lax.broadcasted_iota(I32, (2 * WORDS_PER_PAGE, PAGE_SIZE), 0)
        half = jnp.where(row < WORDS_PER_PAGE, 2 * row, 2 * (row - WORDS_PER_PAGE) + 1)
        in_half = lax.shift_right_logical(slot, np.int32(HALF_BITS.bit_length() - 1)) == half
        bit_value = jnp.where(in_half, lax.shift_left(np.int32(1), slot & np.int32(HALF_BITS - 1)), 0)
        picked = pick(key, idx)
        halves = _dot_nt(bit_value.astype(F32).astype(BF16), picked.astype(BF16)).astype(I32)
        words_ref[...] = halves[:WORDS_PER_PAGE] | lax.shift_left(halves[WORDS_PER_PAGE:], np.int32(HALF_BITS))
        n_picked = jnp.sum(jnp.sum(picked, axis=1, keepdims=True), axis=0, keepdims=True)
        lane = lax.broadcasted_iota(I32, meta_ref.shape, 1)
        meta = jnp.where(lane == META_COUNT, n_picked, jnp.where(lane == META_NEW, pick(key_new, idx_new), 0.0))
        meta_ref[...] = meta.astype(I32)


def dsa_sample_select(iq3, coef3, ikb3, cache_kidx, page_table, layer):
    B, n_pages = page_table.shape
    G = min(PAGES_PER_STEP, n_pages)
    n_keys = n_pages * PAGE_SIZE + 1
    topk = min(TOPK_MAX, n_keys // 4)

    def page_spec(r):
        return pl.BlockSpec((None, None, PAGE_SIZE, IDX_DIM), lambda b, j, pt: (layer, pt[b, j * G + r], 0, 0))

    return pl.pallas_call(
        functools.partial(_dsa_select_body, topk=topk, nbits=n_keys.bit_length()),
        grid_spec=pltpu.PrefetchScalarGridSpec(
            num_scalar_prefetch=1,
            grid=(B, n_pages // G),
            in_specs=[pl.BlockSpec((None, IDX_HEADS, IDX_DIM), lambda b, j, pt: (b, 0, 0)),
                      pl.BlockSpec((None, IDX_HEADS, 1), lambda b, j, pt: (b, 0, 0)),
                      pl.BlockSpec((None, 1, IDX_DIM), lambda b, j, pt: (b, 0, 0))]
                     + [page_spec(r) for r in range(G)],
            out_specs=[pl.BlockSpec((None, WORDS_PER_PAGE, n_pages), lambda b, j, pt: (b, 0, 0)),
                       pl.BlockSpec((None, 1, 128), lambda b, j, pt: (b, 0, 0))],
            scratch_shapes=[pltpu.VMEM((n_pages, PAGE_SIZE), F32)],
        ),
        out_shape=[jax.ShapeDtypeStruct((B, WORDS_PER_PAGE, n_pages), I32),
                   jax.ShapeDtypeStruct((B, 1, 128), I32)],
        compiler_params=_cparams(("parallel", "arbitrary")),
        name="dsa_sample_select",
    )(page_table, iq3, coef3, ikb3, *([cache_kidx] * G))


def _row_copies(ck_ref, cv_ref, kbuf, vbuf, sem, layer, phys, slot, row):
    dst = pl.ds(pl.multiple_of(row * N_HEADS, N_HEADS), N_HEADS)
    return (pltpu.make_async_copy(ck_ref.at[layer, phys, slot], kbuf.at[dst, :], sem.at[0]),
            pltpu.make_async_copy(cv_ref.at[layer, phys, slot], vbuf.at[dst, :], sem.at[1]))


def _dsa_gather_body(pt_ref, ctz_ref, words_ref, meta_ref, q_ref, kn_ref, vn_ref, ck_ref, cv_ref, o_ref,
                     kbuf, vbuf, cnt_ref, sem, *, layer, cap):
    b = pl.program_id(0)
    n_words = words_ref.shape[-1] * WORDS_PER_PAGE
    want = jnp.minimum(meta_ref[0, META_COUNT], cap)
    kbuf[...] = jnp.zeros(kbuf.shape, F32)
    vbuf[...] = jnp.zeros(vbuf.shape, F32)
    cnt_ref[0] = 0

    def scan_word(carry):
        i, _ = carry
        p = lax.shift_right_logical(i, np.int32(WORDS_PER_PAGE.bit_length() - 1))
        w = i & np.int32(WORDS_PER_PAGE - 1)
        word = words_ref[w, p]

        @pl.when(word != 0)
        def _():
            phys = pt_ref[b, p]

            def next_bit(rest):
                low = rest & (-rest)
                bit = ctz_ref[lax.shift_right_logical(low * CTZ_MULTIPLIER, np.int32(27))]
                n = cnt_ref[0]

                @pl.when(n < want)
                def _():
                    for cp in _row_copies(ck_ref, cv_ref, kbuf, vbuf, sem, layer, phys, w * WORD_BITS + bit, n):
                        cp.start()
                    cnt_ref[0] = n + 1

                return rest ^ low

            lax.while_loop(lambda rest: rest != 0, next_bit, word)

        return i + 1, cnt_ref[0]

    _, cnt = lax.while_loop(lambda c: (c[0] < n_words) & (c[1] < want), scan_word, (np.int32(0), np.int32(0)))

    def wait_loop(n, carry):
        for cp in _row_copies(ck_ref, cv_ref, kbuf, vbuf, sem, layer, 0, 0, n):
            cp.wait()
        return carry

    lax.fori_loop(0, cnt, wait_loop, 0)

    q8 = q_ref[...]
    valid = lax.broadcasted_iota(I32, (N_HEADS, cap), 1) < cnt
    seln = meta_ref[0, META_NEW] > 0
    sn = jnp.sum(q8.astype(F32) * kn_ref[...].astype(F32), axis=1, keepdims=True)
    vn = vn_ref[...].astype(F32)
    head = lax.broadcasted_iota(I32, (N_HEADS, HEAD_DIM), 0)
    out = jnp.zeros((N_HEADS, HEAD_DIM), F32)
    for h in range(N_HEADS):
        kh = kbuf[pl.ds(h, cap, stride=N_HEADS), :].astype(BF16)
        vh = vbuf[pl.ds(h, cap, stride=N_HEADS), :].astype(BF16)
        s = _dot_nt(q8, kh)
        m = jnp.maximum(jnp.max(jnp.where(valid, s, NEG_BIG), axis=1, keepdims=True), jnp.where(seln, sn, NEG_BIG))
        p = jnp.where(valid, jnp.exp(s - m), 0.0)
        pn = jnp.where(seln, jnp.exp(sn - m), 0.0)
        l = jnp.sum(p, axis=1, keepdims=True) + pn
        o = (_dot(p.astype(BF16), vh) + pn.astype(BF16).astype(F32) * vn) / l
        out = jnp.where(head == h, o, out)
    o_ref[...] = out.astype(BF16)


def dsa_sample_gather(words, meta, q8, kn8, vn8, cache_k, cache_v, page_table, layer):
    B, n_pages = page_table.shape
    cap = min(TOPK_MAX, (n_pages * PAGE_SIZE + 1) // 4)
    head_rows = pl.BlockSpec((None, N_HEADS, HEAD_DIM), lambda b, pt, ctz: (b, 0, 0))
    return pl.pallas_call(
        functools.partial(_dsa_gather_body, layer=layer, cap=cap),
        grid_spec=pltpu.PrefetchScalarGridSpec(
            num_scalar_prefetch=2,
            grid=(B,),
            in_specs=[pl.BlockSpec((None, WORDS_PER_PAGE, n_pages), lambda b, pt, ctz: (b, 0, 0),
                                   memory_space=pltpu.SMEM),
                      pl.BlockSpec((None, 1, 128), lambda b, pt, ctz: (b, 0, 0), memory_space=pltpu.SMEM),
                      head_rows, head_rows, head_rows,
                      pl.BlockSpec(memory_space=pl.ANY), pl.BlockSpec(memory_space=pl.ANY)],
            out_specs=head_rows,
            scratch_shapes=[pltpu.VMEM((cap * N_HEADS, HEAD_DIM), F32), pltpu.VMEM((cap * N_HEADS, HEAD_DIM), F32),
                            pltpu.SMEM((1,), I32), pltpu.SemaphoreType.DMA((2,))],
        ),
        out_shape=jax.ShapeDtypeStruct((B, N_HEADS, HEAD_DIM), BF16),
        compiler_params=_cparams(("arbitrary",)),
        name="dsa_sample_gather",
    )(page_table, jnp.asarray(CTZ_TABLE), words, meta, q8, kn8, vn8, cache_k, cache_v)


def _merge_body(x_ref, hm_ref, att_ref, ga_ref, gb_ref, wa_ref, wb_ref, wo_ref, o_ref):
    br_a = _dot(hm_ref[...], wa_ref[...])
    br_b = _dot(att_ref[...], wb_ref[...])
    t = jax.nn.sigmoid(ga_ref[...]) * br_a + jax.nn.sigmoid(gb_ref[...]) * br_b
    o_ref[...] = x_ref[...] + _dot(t.astype(BF16), wo_ref[...])


def merge(x, hm, att, z, wa_all, wb_all, wo_all, layer, *, tm):
    M, D = x.shape
    wspec = lambda k: pl.BlockSpec((None, k, D), lambda i: (layer, 0, 0))
    return pl.pallas_call(
        _merge_body,
        grid=(M // tm,),
        in_specs=[pl.BlockSpec((tm, D), lambda i: (i, 0)),
                  pl.BlockSpec((tm, M_V), lambda i: (i, 0)),
                  pl.BlockSpec((tm, A_W), lambda i: (i, 0)),
                  pl.BlockSpec((tm, D), lambda i: (i, C_GA // D)),
                  pl.BlockSpec((tm, D), lambda i: (i, C_GB // D)),
                  wspec(M_V), wspec(A_W), wspec(D)],
        out_specs=pl.BlockSpec((tm, D), lambda i: (i, 0)),
        out_shape=jax.ShapeDtypeStruct((M, D), F32),
        compiler_params=_cparams(("parallel",)),
        name="merge",
    )(x, hm, att, z, z, wa_all, wb_all, wo_all)


def relayout_w_in(w_in, b_in):
    def cols(a):
        mq, mk, mv, mo = a[..., 0:512], a[..., 512:1024], a[..., 1024:2048], a[..., 2048:3072]
        mi, mf = a[..., 3072:3076], a[..., 3076:3080]
        aq, ak, av = a[..., 3080:4104], a[..., 4104:5128], a[..., 5128:6152]
        iq, ik, iw = a[..., 6152:7176], a[..., 7176:7240], a[..., 7240:7256]
        ga, gb = a[..., 7256:9304], a[..., 9304:11352]
        pad = jnp.zeros(a.shape[:-1] + (NZ - C_MISC - L_MF - M_HEADS,), a.dtype)
        return jnp.concatenate([ga, gb, mq, mk, mv, mo, aq, ak, av, iq, ik, iw, mi, mf, pad], axis=-1)

    return cols(w_in).astype(BF16), cols(b_in)[:, None, :]


def _layer(x, layer, P, tabs, *, tm, tf, tn, tq, tab_blocks, transposed, mixers):
    x = ffn(x, P["g_f1"], P["w_f1_gu"], P["w_f1_dn"], layer, tm=tm, tf=tf)
    z = inproj(x, P["g_mix"], P["wr"], P["br"], layer, tm=tm, tn=tn)
    post = qkpost(z, P["g_q"], P["g_k"], P["g_ik"], tabs, layer, tm=tq, tab_blocks=tab_blocks, transposed=transposed)
    kf, vf, miscr = post[1], post[3], post[6]
    hm, att, state = mixers(z, post)
    x = merge(x, hm, att, z, P["w_a"], P["w_b"], P["w_o"], layer, tm=tq)
    x = ffn(x, P["g_f2"], P["w_f2_gu"], P["w_f2_dn"], layer, tm=tm, tf=tf)
    rows = (kf, vf, miscr[:, :IDX_DIM]) + state
    return x, rows


def kernel(x_prompt, x_sample, cache_k, cache_v, cache_kidx, state_C, state_n, state_m, page_table, norm_ffn1, w_ffn1_gu, w_ffn1_down, norm_mix, w_in, b_in, q_norm, k_norm, idx_k_norm, mlstm_norm, w_branch_a, w_branch_b, w_out, norm_ffn2, w_ffn2_gu, w_ffn2_down):
    B, S, D = x_prompt.shape
    BS = x_sample.shape[0]
    depth = w_in.shape[0]
    past = page_table.shape[1] * PAGE_SIZE

    wr, br = relayout_w_in(w_in, b_in)
    pad128 = lambda g: jnp.pad(g, ((0, 0), (0, 128 - g.shape[-1])))[:, None, :]
    P = dict(g_f1=norm_ffn1[:, None, :], w_f1_gu=w_ffn1_gu, w_f1_dn=w_ffn1_down,
             g_mix=norm_mix[:, None, :], wr=wr, br=br,
             g_q=q_norm[:, None, :], g_k=k_norm[:, None, :], g_ik=pad128(idx_k_norm),
             g_m=mlstm_norm.reshape(depth, 1, M_V),
             w_a=w_branch_a.astype(BF16), w_b=w_branch_b.astype(BF16), w_o=w_out.astype(BF16),
             g_f2=norm_ffn2[:, None, :], w_f2_gu=w_ffn2_gu, w_f2_dn=w_ffn2_down)
    tabs_p = rope_tables(jnp.arange(S))
    tabs_s = rope_tables(jnp.full((BS,), past, jnp.int32))

    xp = x_prompt.reshape(B * S, D)
    xs = x_sample.reshape(BS, D)
    rows_p, rows_s = [], []
    for layer in range(depth):
        def mix_prompt(z, post):
            qt, _, kb, _, vt, iqt, miscr, misct = post
            hm, C, n, m = mlstm_prompt(z, P["g_m"], layer, B=B, S=S, T=256)
            att = dsa_prompt(qt, iqt, misct, miscr, kb, vt, B=B, S=S)
            return hm, att, (C, n, m)

        def mix_sample(z, post):
            qb, _, kb, _, vb, iqb, miscr = post
            hm, C, n, m = mlstm_step(z, P["g_m"], state_C, state_n, state_m, layer)
            iq3 = iqb.reshape(BS, IDX_HEADS, IDX_DIM)
            coef3 = miscr[:, L_IW:L_IW + IDX_HEADS].reshape(BS, IDX_HEADS, 1)
            ikb3 = miscr[:, :IDX_DIM].astype(BF16).reshape(BS, 1, IDX_DIM)
            words, meta = dsa_sample_select(iq3, coef3, ikb3, cache_kidx, page_table, layer)
            heads = lambda a: a.reshape(BS, N_HEADS, HEAD_DIM)
            att = dsa_sample_gather(words, meta, heads(qb), heads(kb), heads(vb), cache_k, cache_v, page_table, layer)
            return hm, att.reshape(BS, A_W), (C, n, m)

        xp, rp = _layer(xp, layer, P, tabs_p, tm=1024, tf=256, tn=1280, tq=512, tab_blocks=S // 512,
                        transposed=True, mixers=mix_prompt)
        xs, rs = _layer(xs, layer, P, tabs_s, tm=BS, tf=512, tn=1280, tq=BS, tab_blocks=1,
                        transposed=False, mixers=mix_sample)
        rows_p.append(rp)
        rows_s.append(rs)

    def stack(rows, idx, shape):
        return jnp.stack([r[idx] for r in rows]).reshape((depth,) + shape)

    return (xp.reshape(B, S, D), xs.reshape(BS, 1, D),
            stack(rows_p, 0, (B, S, N_HEADS, HEAD_DIM)), stack(rows_p, 1, (B, S, N_HEADS, HEAD_DIM)),
            stack(rows_p, 2, (B, S, IDX_DIM)),
            stack(rows_p, 3, (B, M_HEADS, M_QK_DIM, M_V_DIM)), stack(rows_p, 4, (B, M_HEADS, M_QK_DIM)),
            stack(rows_p, 5, (B, M_HEADS)),
            stack(rows_s, 0, (BS, 1, N_HEADS, HEAD_DIM)), stack(rows_s, 1, (BS, 1, N_HEADS, HEAD_DIM)),
            stack(rows_s, 2, (BS, 1, IDX_DIM)),
            stack(rows_s, 3, (BS, M_HEADS, M_QK_DIM, M_V_DIM)), stack(rows_s, 4, (BS, M_HEADS, M_QK_DIM)),
            stack(rows_s, 5, (BS, M_HEADS)))
```

```python
import functools

import jax
import jax.numpy as jnp
import numpy as np
from jax import lax
from jax.experimental import pallas as pl
from jax.experimental.pallas import tpu as pltpu

F32 = jnp.float32
BF16 = jnp.bfloat16
I32 = jnp.int32

EPS = 1e-6
ROPE_THETA = 500000.0
PAGE_SIZE = 128
TOPK_MAX = 256

M_HEADS = 4
M_QK_DIM = 128
M_V_DIM = 256
N_HEADS = 8
HEAD_DIM = 128
IDX_HEADS = 16
IDX_DIM = 64
D_MODEL = 2048

M_QK = M_HEADS * M_QK_DIM
M_V = M_HEADS * M_V_DIM
A_W = N_HEADS * HEAD_DIM
IDX_QW = IDX_HEADS * IDX_DIM

C_GA = 0
C_GB = C_GA + D_MODEL
C_MQ = C_GB + D_MODEL
C_MK = C_MQ + M_QK
C_MV = C_MK + M_QK
C_MO = C_MV + M_V
NZ_G = C_MO + M_V
C_AQ = 0
C_AK = C_AQ + A_W
C_AV = C_AK + A_W
C_IQ = C_AV + A_W
C_MISC = C_IQ + IDX_QW
L_IK = 0
L_IW = IDX_DIM
L_MI = L_IW + IDX_HEADS
L_MF = L_MI + M_HEADS
MISC_W = 128
NZ_A = C_MISC + 2 * MISC_W

VMEM_LIMIT = 58 * 1024 * 1024

NEG_BIG = -1e30
MAX_INIT = -1e20
KEY_TILE = 256


def _cparams(sem, vmem=VMEM_LIMIT):
    return pltpu.CompilerParams(dimension_semantics=sem, vmem_limit_bytes=vmem)


def _rms(x, g):
    return x * lax.rsqrt(jnp.mean(x * x, axis=-1, keepdims=True) + EPS) * g


def _dot(a, b):
    return jnp.dot(a, b, preferred_element_type=F32)


def _dot_nt(a, b):
    return lax.dot_general(a, b, (((1,), (1,)), ((), ())), preferred_element_type=F32)


def _log_sigmoid(x):
    return jnp.minimum(x, 0.0) - jnp.log1p(jnp.exp(-jnp.abs(x)))


def _ffn_body(x_ref, xs_ref, g_ref, wg_ref, wu_ref, wd_ref, o_ref, os_ref, h_ref):
    i, j = pl.program_id(0), pl.program_id(1)
    tm = x_ref.shape[0]

    @pl.when(j == 0)
    def _():
        x = x_ref[...]
        h_ref[:tm, :] = _rms(x, g_ref[...]).astype(BF16)
        o_ref[...] = x

    @pl.when((i == 0) & (j == 0))
    def _():
        xs = xs_ref[...]
        h_ref[tm:, :] = _rms(xs, g_ref[...]).astype(BF16)
        os_ref[...] = xs

    h = h_ref[...]
    a = _dot(h, wg_ref[...].astype(BF16))
    u = _dot(h, wu_ref[...].astype(BF16))
    p = (0.5 * a) * jax.nn.sigmoid(a) * u
    r = _dot(p.astype(BF16), wd_ref[...].astype(BF16))
    o_ref[...] += r[:tm]

    @pl.when(i == 0)
    def _():
        os_ref[...] += r[tm:]


def ffn(x, xs, g_all, wgu_all, wdn_all, layer, *, tm, tf):
    M, D = x.shape
    MS = xs.shape[0]
    F = wdn_all.shape[1]
    nf = F // tf
    return pl.pallas_call(
        _ffn_body,
        grid=(M // tm, nf),
        in_specs=[
            pl.BlockSpec((tm, D), lambda i, j: (i, 0)),
            pl.BlockSpec((MS, D), lambda i, j: (0, 0)),
            pl.BlockSpec((None, 1, D), lambda i, j: (layer, 0, 0)),
            pl.BlockSpec((None, D, tf), lambda i, j: (layer, 0, j)),
            pl.BlockSpec((None, D, tf), lambda i, j: (layer, 0, j + nf)),
            pl.BlockSpec((None, tf, D), lambda i, j: (layer, j, 0)),
        ],
        out_specs=[pl.BlockSpec((tm, D), lambda i, j: (i, 0)),
                   pl.BlockSpec((MS, D), lambda i, j: (0, 0))],
        out_shape=[jax.ShapeDtypeStruct((M, D), F32), jax.ShapeDtypeStruct((MS, D), F32)],
        scratch_shapes=[pltpu.VMEM((tm + MS, D), BF16)],
        compiler_params=_cparams(("arbitrary", "arbitrary")),
        name="ffn",
    )(x, xs, g_all, wgu_all, wgu_all, wdn_all)


def _inproj_body(x_ref, g_ref, w_ref, b_ref, z_ref, h_ref):
    @pl.when(pl.program_id(1) == 0)
    def _():
        h_ref[...] = _rms(x_ref[...], g_ref[...]).astype(BF16)

    z_ref[...] = (_dot(h_ref[...], w_ref[...]) + b_ref[...]).astype(z_ref.dtype)


def inproj(x, g_all, wr_all, br_all, layer, *, tm, tn, out_dtype):
    M, D = x.shape
    NZ = wr_all.shape[-1]
    return pl.pallas_call(
        _inproj_body,
        grid=(M // tm, NZ // tn),
        in_specs=[
            pl.BlockSpec((tm, D), lambda i, j: (i, 0)),
            pl.BlockSpec((None, 1, D), lambda i, j: (layer, 0, 0)),
            pl.BlockSpec((None, D, tn), lambda i, j: (layer, 0, j)),
            pl.BlockSpec((None, 1, tn), lambda i, j: (layer, 0, j)),
        ],
        out_specs=pl.BlockSpec((tm, tn), lambda i, j: (i, j)),
        out_shape=jax.ShapeDtypeStruct((M, NZ), out_dtype),
        scratch_shapes=[pltpu.VMEM((tm, D), BF16)],
        compiler_params=_cparams(("parallel", "arbitrary")),
        name="inproj",
    )(x, g_all, wr_all, br_all)


def _rope(x, c, sa, sb, shift):
    n = x.shape[-1]
    return x * c + pltpu.roll(x, n - shift, 1) * sa + pltpu.roll(x, shift, 1) * sb


def _qkpost_body(*refs, transposed, n_carried):
    (aq_ref, ak_ref, av_ref, iq_ref, misc_ref, gq_ref, gk_ref, gik_ref,
     c128_ref, sa128_ref, sb128_ref, c64_ref, sa64_ref, sb64_ref) = refs[:14]
    q_ref, kf_ref, kb_ref, vf_ref, v_ref, iqo_ref, miscr_ref, *misct_ref = refs[14 + n_carried:]
    c128, sa128, sb128 = c128_ref[...], sa128_ref[...], sb128_ref[...]
    c64, sa64, sb64 = c64_ref[...], sa64_ref[...], sb64_ref[...]
    gq, gk = gq_ref[...], gk_ref[...]
    half128 = HEAD_DIM // 8
    half64 = IDX_DIM // 8
    tm = aq_ref.shape[0]
    for h in range(N_HEADS):
        sl = slice(h * HEAD_DIM, (h + 1) * HEAD_DIM)
        q = _rope(_rms(aq_ref[:, sl], gq), c128, sa128, sb128, half128) * (HEAD_DIM ** -0.5)
        k = _rope(_rms(ak_ref[:, sl], gk), c128, sa128, sb128, half128)
        kf_ref[:, h, :] = k
        kb_ref[:, sl] = k.astype(BF16)
        vf_ref[:, h, :] = av_ref[:, sl]
        if transposed:
            q_ref[sl, :] = q.T.astype(BF16)
            for t in range(tm // KEY_TILE):
                v_ref[t, sl, :] = av_ref[t * KEY_TILE:(t + 1) * KEY_TILE, sl].T.astype(BF16)
        else:
            q_ref[:, sl] = q.astype(BF16)
            v_ref[:, sl] = av_ref[:, sl].astype(BF16)
    for h in range(IDX_QW // 128):
        sl = slice(h * 128, (h + 1) * 128)
        iq = _rope(iq_ref[:, sl], c64, sa64, sb64, half64)
        if transposed:
            iqo_ref[sl, :] = iq.T.astype(BF16)
        else:
            iqo_ref[:, sl] = iq.astype(BF16)
    misc = misc_ref[...]
    lane = lax.broadcasted_iota(I32, misc.shape, 1)
    is_ik = lane < IDX_DIM
    ikv = jnp.where(is_ik, misc, 0.0)
    ms = jnp.sum(ikv * ikv, axis=-1, keepdims=True) * (1.0 / IDX_DIM)
    ikn = ikv * lax.rsqrt(ms + EPS) * gik_ref[...]
    ikr = _rope(ikn, c64, sa64, sb64, half64)
    is_iw = (lane >= L_IW) & (lane < L_IW + IDX_HEADS)
    iw_scale = (IDX_HEADS ** -0.5) * (IDX_DIM ** -0.5)
    miscr = jnp.where(is_ik, ikr, jnp.where(is_iw, misc * iw_scale, misc))
    miscr_ref[...] = miscr
    if transposed:
        misct_ref[0][...] = miscr.T


def qkpost(z, gq_all, gk_all, gik_all, tabs, layer, carried, *, depth, tm, tab_blocks, transposed):
    M = z.shape[0]
    zspec = lambda w, c: pl.BlockSpec((tm, w), lambda i: (i, c // w))
    tspec = pl.BlockSpec((tm, 128), lambda i: (i % tab_blocks, 0))
    gspec = pl.BlockSpec((None, 1, 128), lambda i: (layer, 0, 0))
    row = lambda w: pl.BlockSpec((tm, w), lambda i: (i, 0))
    col = lambda w: pl.BlockSpec((w, tm), lambda i: (0, i))
    rows_spec = pl.BlockSpec((None, tm, N_HEADS, HEAD_DIM), lambda i: (layer, i, 0, 0))
    sds = jax.ShapeDtypeStruct
    rows_shape = sds((depth, M, N_HEADS, HEAD_DIM), F32)
    if transposed:
        nt = tm // KEY_TILE
        out_specs = [col(A_W), rows_spec, row(A_W), rows_spec, pl.BlockSpec((nt, A_W, KEY_TILE), lambda i: (i, 0, 0)),
                     col(IDX_QW), row(MISC_W), col(MISC_W)]
        out_shape = [sds((A_W, M), BF16), rows_shape, sds((M, A_W), BF16), rows_shape,
                     sds((M // KEY_TILE, A_W, KEY_TILE), BF16), sds((IDX_QW, M), BF16), sds((M, MISC_W), F32),
                     sds((MISC_W, M), F32)]
    else:
        out_specs = [row(A_W), rows_spec, row(A_W), rows_spec, row(A_W), row(IDX_QW), row(MISC_W)]
        out_shape = [sds((M, A_W), BF16), rows_shape, sds((M, A_W), BF16), rows_shape,
                     sds((M, A_W), BF16), sds((M, IDX_QW), BF16), sds((M, MISC_W), F32)]
    n_in = 14
    return pl.pallas_call(
        functools.partial(_qkpost_body, transposed=transposed, n_carried=len(carried)),
        grid=(M // tm,),
        in_specs=[zspec(A_W, C_AQ), zspec(A_W, C_AK), zspec(A_W, C_AV), zspec(IDX_QW, C_IQ),
                  zspec(MISC_W, C_MISC), gspec, gspec, gspec] + [tspec] * 6
                 + [pl.BlockSpec(memory_space=pl.ANY)] * len(carried),
        out_specs=out_specs,
        out_shape=out_shape,
        input_output_aliases={n_in + c: o for c, o in zip(range(len(carried)), (1, 3))},
        compiler_params=_cparams(("parallel",)),
        name="qkpost",
    )(z, z, z, z, z, gq_all, gk_all, gik_all, *tabs, *carried)


def rope_tables(pos):
    pos = pos.astype(F32)[:, None]

    def tables(width):
        rd = width // 4
        half = rd // 2
        inv = ROPE_THETA ** (-jnp.arange(half, dtype=F32) * (2.0 / rd))
        lane = np.arange(128) % width
        ang = pos * inv[None, :][:, lane % half]
        cos, sin = jnp.cos(ang), jnp.sin(ang)
        lo = jnp.asarray(lane < half)[None, :]
        hi = jnp.asarray((lane >= half) & (lane < rd))[None, :]
        c = jnp.where(lo | hi, cos, 1.0)
        sa = jnp.where(lo, -sin, 0.0)
        sb = jnp.where(hi, sin, 0.0)
        return c, sa, sb

    return tables(HEAD_DIM) + tables(IDX_DIM)


def _split3(x):
    hi = x.astype(BF16)
    r1 = x - hi.astype(F32)
    mid = r1.astype(BF16)
    lo = (r1 - mid.astype(F32)).astype(BF16)
    return hi, mid, lo


def _mlstm_body(zq_ref, zk_ref, zv_ref, zo_ref, misc_ref, gm_ref, s0_ref, hm_ref, sout_ref, s_ref, m_ref):
    c = pl.program_id(1)
    T = zq_ref.shape[0]
    EW = s_ref.shape[-1]

    @pl.when(c == 0)
    def _():
        s_ref[...] = s0_ref[...]
        for h in range(M_HEADS):
            m_ref[h:h + 1, :] = jnp.broadcast_to(s0_ref[h, 0:1, M_V_DIM + 1:M_V_DIM + 2], (1, 128))

    misc = misc_ref[...]
    misc_t = misc.T
    gates_t = misc_t[L_MI:L_MI + 8, :]
    lf_rows = _log_sigmoid(gates_t)
    lf_cols = _log_sigmoid(misc)
    ti = lax.broadcasted_iota(I32, (T, T), 0)
    si = lax.broadcasted_iota(I32, (T, T), 1)
    causal = si <= ti
    ltri = jnp.where(causal, 1.0, 0.0).astype(BF16)
    utri = jnp.where(ti <= si, 1.0, 0.0).astype(BF16)
    b_cols = sum(_dot(ltri, p) for p in _split3(lf_cols))
    b_rows = sum(_dot(p, utri) for p in _split3(lf_rows))
    ones_col = jnp.where(lax.broadcasted_iota(I32, (T, EW - M_V_DIM), 1) == 0, 1.0, 0.0).astype(BF16)

    for h in range(M_HEADS):
        q = (zq_ref[:, h * M_QK_DIM:(h + 1) * M_QK_DIM].astype(F32) * (M_QK_DIM ** -0.5)).astype(BF16)
        k = zk_ref[:, h * M_QK_DIM:(h + 1) * M_QK_DIM].astype(F32)
        ve = jnp.concatenate([zv_ref[:, h * M_V_DIM:(h + 1) * M_V_DIM].astype(BF16), ones_col], axis=1)
        bcol = b_cols[:, L_MF + h:L_MF + h + 1]
        brow = b_rows[M_HEADS + h:M_HEADS + h + 1, :]
        r = gates_t[h:h + 1, :] - brow
        m0 = m_ref[h:h + 1, 0:1]
        d = jnp.where(causal, bcol + r, -jnp.inf)
        inter = bcol + m0
        m = jnp.maximum(inter, jnp.max(d, axis=1, keepdims=True))
        w = (_dot_nt(q, k.astype(BF16)) * jnp.exp(d - m)).astype(BF16)
        c_inter = jnp.exp(inter - m)
        s0 = s_ref[h]
        nd = c_inter * _dot(q, s0.astype(BF16)) + _dot(w, ve)
        hh = nd[:, :M_V_DIM] / jnp.maximum(jnp.abs(nd[:, M_V_DIM:M_V_DIM + 1]), jnp.exp(-m))
        sl = slice(h * M_V_DIM, (h + 1) * M_V_DIM)
        hm_ref[:, sl] = (_rms(hh, gm_ref[:, sl]) * jax.nn.sigmoid(zo_ref[:, sl].astype(F32))).astype(BF16)

        b_end = brow[:, T - 1:T]
        g = b_end + r
        m_new = jnp.maximum(b_end + m0, jnp.max(g, axis=1, keepdims=True))
        a = jnp.exp(g - m_new)
        c_old = jnp.exp(b_end + m0 - m_new)
        s_ref[h] = c_old * s0 + _dot((k.T * a).astype(BF16), ve)
        m_ref[h:h + 1, :] = jnp.broadcast_to(m_new, (1, 128))

    @pl.when(c == pl.num_programs(1) - 1)
    def _():
        ri = lax.broadcasted_iota(I32, s_ref.shape[1:], 0)
        ci = lax.broadcasted_iota(I32, s_ref.shape[1:], 1)
        for h in range(M_HEADS):
            sout_ref[h] = jnp.where((ri == 0) & (ci == M_V_DIM + 1), m_ref[h:h + 1, 0:1], s_ref[h])


def mlstm_prompt(zg, za, gm_all, layer, *, B, S, T):
    EW = M_V_DIM + 128
    nc = S // T
    s0 = jnp.zeros((M_HEADS, M_QK_DIM, EW), F32)
    zspec = lambda w, c0: pl.BlockSpec((T, w), lambda b, c: (b * nc + c, c0 // w))
    hm, sout = pl.pallas_call(
        _mlstm_body,
        grid=(B, nc),
        in_specs=[zspec(M_QK, C_MQ), zspec(M_QK, C_MK), zspec(M_V, C_MV), zspec(M_V, C_MO), zspec(MISC_W, C_MISC),
                  pl.BlockSpec((None, 1, M_V), lambda b, c: (layer, 0, 0)),
                  pl.BlockSpec((M_HEADS, M_QK_DIM, EW), lambda b, c: (0, 0, 0))],
        out_specs=[pl.BlockSpec((T, M_V), lambda b, c: (b * nc + c, 0)),
                   pl.BlockSpec((None, M_HEADS, M_QK_DIM, EW), lambda b, c: (b, 0, 0, 0))],
        out_shape=[jax.ShapeDtypeStruct((B * S, M_V), BF16),
                   jax.ShapeDtypeStruct((B, M_HEADS, M_QK_DIM, EW), F32)],
        scratch_shapes=[pltpu.VMEM((M_HEADS, M_QK_DIM, EW), F32), pltpu.VMEM((8, 128), F32)],
        compiler_params=_cparams(("parallel", "arbitrary")),
        name="mlstm_prompt",
    )(zg, zg, zg, zg, za, gm_all, s0)
    return hm, sout[..., :M_V_DIM], sout[..., M_V_DIM], sout[:, :, 0, M_V_DIM + 1]


def _mlstm_step_body(zq_ref, zk_ref, zv_ref, zo_ref, misc_ref, gm_ref, c_ref, n_ref, m_ref,
                     hm_ref, cout_ref, nout_ref, mout_ref):
    misc = misc_ref[...]
    lf_all = _log_sigmoid(misc)
    eye = lax.broadcasted_iota(I32, (M_QK_DIM, M_QK_DIM), 0) == lax.broadcasted_iota(I32, (M_QK_DIM, M_QK_DIM), 1)
    for h in range(M_HEADS):
        q = zq_ref[:, h * M_QK_DIM:(h + 1) * M_QK_DIM].astype(F32) * (M_QK_DIM ** -0.5)
        k = zk_ref[:, h * M_QK_DIM:(h + 1) * M_QK_DIM].astype(F32)
        sl = slice(h * M_V_DIM, (h + 1) * M_V_DIM)
        v = zv_ref[:, sl].astype(F32)
        li = misc[:, L_MI + h:L_MI + h + 1]
        lf = lf_all[:, L_MF + h:L_MF + h + 1]
        s0 = c_ref[h]
        n0 = n_ref[h:h + 1, :]
        m0 = m_ref[:, h:h + 1]
        m = jnp.maximum(lf + m0, li)
        w = jnp.sum(q * k, axis=1, keepdims=True) * jnp.exp(li - m)
        c_inter = jnp.exp(lf + m0 - m)
        q8 = jnp.broadcast_to(q, (8, M_QK_DIM)).astype(BF16)
        qs = _dot(q8, s0.astype(BF16))[0:1, :]
        num = c_inter * qs + w * v
        den = c_inter * jnp.sum(q * n0, axis=1, keepdims=True) + w
        hh = num / jnp.maximum(jnp.abs(den), jnp.exp(-m))
        hm_ref[:, sl] = (_rms(hh, gm_ref[:, sl]) * jax.nn.sigmoid(zo_ref[:, sl].astype(F32))).astype(BF16)
        a = jnp.exp(li - m)
        kcol = jnp.sum(jnp.where(eye, jnp.broadcast_to(k, (M_QK_DIM, M_QK_DIM)), 0.0), axis=1, keepdims=True)
        cout_ref[h] = c_inter * s0 + (a * kcol) * v
        nout_ref[h:h + 1, :] = c_inter * n0 + a * k
        mout_ref[:, h:h + 1] = m


def mlstm_step(zg, za, gm_all, state_c, state_n, state_m, layer):
    B = zg.shape[0]
    zg3 = zg.reshape(B, 1, NZ_G)
    za3 = za.reshape(B, 1, NZ_A)
    m4 = state_m.reshape(state_m.shape[0], B, 1, M_HEADS)
    zspec = lambda w, c0: pl.BlockSpec((None, 1, w), lambda b: (b, 0, c0 // w))
    hm, cout, nout, mout = pl.pallas_call(
        _mlstm_step_body,
        grid=(B,),
        in_specs=[zspec(M_QK, C_MQ), zspec(M_QK, C_MK), zspec(M_V, C_MV), zspec(M_V, C_MO), zspec(MISC_W, C_MISC),
                  pl.BlockSpec((None, 1, M_V), lambda b: (layer, 0, 0)),
                  pl.BlockSpec((None, None, M_HEADS, M_QK_DIM, M_V_DIM), lambda b: (layer, b, 0, 0, 0)),
                  pl.BlockSpec((None, None, M_HEADS, M_QK_DIM), lambda b: (layer, b, 0, 0)),
                  pl.BlockSpec((None, None, 1, M_HEADS), lambda b: (layer, b, 0, 0))],
        out_specs=[pl.BlockSpec((None, 1, M_V), lambda b: (b, 0, 0)),
                   pl.BlockSpec((None, M_HEADS, M_QK_DIM, M_V_DIM), lambda b: (b, 0, 0, 0)),
                   pl.BlockSpec((None, M_HEADS, M_QK_DIM), lambda b: (b, 0, 0)),
                   pl.BlockSpec((None, 1, M_HEADS), lambda b: (b, 0, 0))],
        out_shape=[jax.ShapeDtypeStruct((B, 1, M_V), BF16),
                   jax.ShapeDtypeStruct((B, M_HEADS, M_QK_DIM, M_V_DIM), F32),
                   jax.ShapeDtypeStruct((B, M_HEADS, M_QK_DIM), F32),
                   jax.ShapeDtypeStruct((B, 1, M_HEADS), F32)],
        compiler_params=_cparams(("parallel",)),
        name="mlstm_step",
    )(zg3, zg3, zg3, zg3, za3, gm_all, state_c, state_n, m4)
    return hm.reshape(B, M_V), cout, nout, mout.reshape(B, M_HEADS)


_INT_MIN = np.int32(-2 ** 31)
_KEY_NEG_INF = np.int32(np.array(0xFF800000, np.uint32).view(np.int32) ^ np.int32(0x7FFFFFFF))


def _sort_key(x):
    bits = pltpu.bitcast(x, I32)
    return jnp.where(bits < 0, bits ^ np.int32(0x7FFFFFFF), bits)


def _topk_select(count, topk, nbits, shape):
    def bit_step(b, theta_u):
        cand_u = theta_u | lax.shift_left(np.int32(1), np.int32(31) - b)
        cand_s = cand_u ^ _INT_MIN
        cnt = count(lambda key, idx: jnp.where(key >= cand_s, 1.0, 0.0))
        return jnp.where(cnt >= float(topk), cand_u, theta_u)

    theta = lax.fori_loop(0, 32, bit_step, jnp.zeros(shape, I32)) ^ _INT_MIN
    room = float(topk) - count(lambda key, idx: jnp.where(key > theta, 1.0, 0.0))

    def tie_step(b, jlim):
        cand = jlim | lax.shift_left(np.int32(1), np.int32(nbits - 1) - b)
        f = count(lambda key, idx: jnp.where(key == theta, jnp.where(idx < cand, 1.0, 0.0), 0.0))
        return jnp.where(f <= room, cand, jlim)

    return theta, room, tie_step


def _dsa_prompt_body(qt_ref, iqt_ref, misct_ref, k_ref, vt_ref, misck_ref, o_ref,
                     key_ref, jlim_ref, m_ref, l_ref, alpha_ref, acc_ref, bias_ref, s_ref, p_ref, *, topk, nbits):
    i = pl.program_id(1)
    KT, QB = key_ref.shape[1:]
    nt = i + 1
    kloc = lax.broadcasted_iota(I32, (KT, QB), 0)
    qpos = i * QB + lax.broadcasted_iota(I32, (KT, QB), 1)
    coef = misct_ref[L_IW:L_IW + IDX_HEADS, :]

    def score_tile(j, carry):
        off = pl.multiple_of(j * KT, KT)
        ik = misck_ref[pl.ds(off, KT), :][:, :IDX_DIM].astype(BF16)
        sc = jnp.zeros((KT, QB), F32)
        for h in range(IDX_HEADS):
            d = _dot(ik, iqt_ref[h * IDX_DIM:(h + 1) * IDX_DIM, :])
            sc = sc + jnp.maximum(d, 0.0) * coef[h:h + 1, :]
        sc = jnp.where(j * KT + kloc <= qpos, sc, -jnp.inf)
        key_ref[j] = _sort_key(sc)
        return carry

    lax.fori_loop(0, nt, score_tile, 0)

    def count(f):
        def body(j, cnt8):
            return cnt8 + jnp.sum(f(key_ref[j], j * KT + kloc).reshape(KT // 8, 8, QB), axis=0)

        return jnp.sum(lax.fori_loop(0, nt, body, jnp.zeros((8, QB), F32)), axis=0, keepdims=True)

    theta, room, tie_step = _topk_select(count, topk, nbits, (1, QB))
    n_tie = count(lambda key, idx: jnp.where(key == theta, 1.0, 0.0))
    jlim_ref[...] = jnp.full((1, QB), 2 ** nbits, I32)
    crowded = jnp.max(jnp.where((n_tie > room) & (theta > _KEY_NEG_INF), 1.0, 0.0))

    @pl.when(crowded > 0.0)
    def _():
        jlim_ref[...] = lax.fori_loop(0, nbits, tie_step, jnp.zeros((1, QB), I32))

    jlim = jlim_ref[...]

    def finalize(j, carry):
        key = key_ref[j]
        sel = jnp.where(key > theta, 1, jnp.where(key == theta, jnp.where(j * KT + kloc < jlim, 1, 0), 0))
        key_ref[j] = jnp.where(key > _KEY_NEG_INF, sel, 0)
        return carry

    lax.fori_loop(0, nt, finalize, 0)

    m_ref[...] = jnp.full(m_ref.shape, MAX_INIT, F32)
    l_ref[...] = jnp.zeros(l_ref.shape, F32)
    acc_ref[...] = jnp.zeros(acc_ref.shape, F32)

    def att_tile(j, carry):
        off = pl.multiple_of(j * KT, KT)
        bias_ref[...] = jnp.where(key_ref[j] > 0, 0.0, NEG_BIG)
        for h in range(N_HEADS):
            sl = slice(h * HEAD_DIM, (h + 1) * HEAD_DIM)
            s_ref[h] = _dot(k_ref[pl.ds(off, KT), sl], qt_ref[sl, :]) + bias_ref[...]
        for h in range(N_HEADS):
            m_old = m_ref[h:h + 1, :]
            m_new = jnp.maximum(m_old, jnp.max(s_ref[h], axis=0, keepdims=True))
            p = jnp.exp(s_ref[h] - m_new)
            alpha = jnp.exp(m_old - m_new)
            l_ref[h:h + 1, :] = alpha * l_ref[h:h + 1, :] + jnp.sum(p, axis=0, keepdims=True)
            p_ref[h] = p.astype(BF16)
            alpha_ref[h:h + 1, :] = alpha
            m_ref[h:h + 1, :] = m_new
        for h in range(N_HEADS):
            sl = slice(h * HEAD_DIM, (h + 1) * HEAD_DIM)
            acc_ref[sl, :] = alpha_ref[h:h + 1, :] * acc_ref[sl, :] + _dot(vt_ref[j, sl, :], p_ref[h])
        return carry

    lax.fori_loop(0, nt, att_tile, 0)
    for h in range(N_HEADS):
        sl = slice(h * HEAD_DIM, (h + 1) * HEAD_DIM)
        o_ref[:, sl] = (acc_ref[sl, :] / l_ref[h:h + 1, :]).T.astype(BF16)


def dsa_prompt(qt, iqt, misct, miscr, kb, vt, *, B, S):
    QB = KEY_TILE
    topk = min(TOPK_MAX, S // 4)
    nq = S // QB
    col = lambda w: pl.BlockSpec((w, QB), lambda b, i: (0, b * nq + i))
    return pl.pallas_call(
        functools.partial(_dsa_prompt_body, topk=topk, nbits=S.bit_length()),
        grid=(B, nq),
        in_specs=[col(A_W), col(IDX_QW), col(MISC_W),
                  pl.BlockSpec((S, A_W), lambda b, i: (b, 0)),
                  pl.BlockSpec((nq, A_W, KEY_TILE), lambda b, i: (b, 0, 0)),
                  pl.BlockSpec((S, MISC_W), lambda b, i: (b, 0))],
        out_specs=pl.BlockSpec((QB, A_W), lambda b, i: (b * nq + i, 0)),
        out_shape=jax.ShapeDtypeStruct((B * S, A_W), BF16),
        scratch_shapes=[pltpu.VMEM((nq, KEY_TILE, QB), I32), pltpu.VMEM((1, QB), I32),
                        pltpu.VMEM((N_HEADS, QB), F32), pltpu.VMEM((N_HEADS, QB), F32), pltpu.VMEM((N_HEADS, QB), F32),
                        pltpu.VMEM((A_W, QB), F32), pltpu.VMEM((KEY_TILE, QB), F32),
                        pltpu.VMEM((N_HEADS, KEY_TILE, QB), F32), pltpu.VMEM((N_HEADS, KEY_TILE, QB), BF16)],
        compiler_params=_cparams(("parallel", "arbitrary")),
        name="dsa_prompt",
    )(qt, iqt, misct, kb, vt, miscr)


PAGES_PER_STEP = 16
WORD_BITS = 32
HALF_BITS = WORD_BITS // 2
WORDS_PER_PAGE = PAGE_SIZE // WORD_BITS
CTZ_MULTIPLIER = np.int32(0x077CB531)
CTZ_TABLE = np.array([0, 1, 28, 2, 29, 14, 24, 3, 30, 22, 20, 15, 25, 17, 4, 8,
                      31, 27, 13, 23, 21, 19, 16, 7, 26, 12, 18, 6, 11, 5, 10, 9], np.int32)
META_COUNT = 0
META_NEW = 1


def _dsa_select_body(pt_ref, iq_ref, coef_ref, ikn_ref, *rest, topk, nbits):
    page_refs = rest[:-3]
    words_ref, meta_ref, sc_ref = rest[-3:]
    G = len(page_refs)
    j = pl.program_id(1)
    iq = iq_ref[...]
    coef = coef_ref[...]
    for r, kref in enumerate(page_refs):
        d = _dot_nt(iq, kref[...].astype(BF16))
        sc_ref[pl.ds(j * G + r, 1), :] = jnp.sum(jnp.maximum(d, 0.0) * coef, axis=0, keepdims=True)

    @pl.when(j == pl.num_programs(1) - 1)
    def _():
        n_pages = sc_ref.shape[0]
        dn = jnp.sum(iq.astype(F32) * ikn_ref[...].astype(F32), axis=1, keepdims=True)
        key_new = _sort_key(jnp.sum(jnp.maximum(dn, 0.0) * coef, axis=0, keepdims=True))
        key = _sort_key(sc_ref[...])
        idx = (lax.broadcasted_iota(I32, key.shape, 0) * PAGE_SIZE + lax.broadcasted_iota(I32, key.shape, 1))
        idx_new = jnp.full((1, 1), n_pages * PAGE_SIZE, I32)

        def count(f):
            past = jnp.sum(jnp.sum(f(key, idx), axis=1, keepdims=True), axis=0, keepdims=True)
            return past + f(key_new, idx_new)

        theta, room, tie_step = _topk_select(count, topk, nbits, (1, 1))
        jlim = lax.fori_loop(0, nbits, tie_step, jnp.zeros((1, 1), I32))
        pick = lambda k, ix: jnp.where(k > theta, 1.0, jnp.where(k == theta, jnp.where(ix < jlim, 1.0, 0.0), 0.0))
        slot = lax.broadcasted_iota(I32, (2 * WORDS_PER_PAGE, PAGE_SIZE), 1)
        row = lax.broadcasted_iota(I32, (2 * WORDS_PER_PAGE, PAGE_SIZE), 0)
        half = jnp.where(row < WORDS_PER_PAGE, 2 * row, 2 * (row - WORDS_PER_PAGE) + 1)
        in_half = lax.shift_right_logical(slot, np.int32(HALF_BITS.bit_length() - 1)) == half
        bit_value = jnp.where(in_half, lax.shift_left(np.int32(1), slot & np.int32(HALF_BITS - 1)), 0)
        picked = pick(key, idx)
        halves = _dot_nt(bit_value.astype(F32).astype(BF16), picked.astype(BF16)).astype(I32)
        words_ref[...] = halves[:WORDS_PER_PAGE] | lax.shift_left(halves[WORDS_PER_PAGE:], np.int32(HALF_BITS))
        n_picked = jnp.sum(jnp.sum(picked, axis=1, keepdims=True), axis=0, keepdims=True)
        lane = lax.broadcasted_iota(I32, meta_ref.shape, 1)
        meta = jnp.where(lane == META_COUNT, n_picked, jnp.where(lane == META_NEW, pick(key_new, idx_new), 0.0))
        meta_ref[...] = meta.astype(I32)


def dsa_sample_select(iq3, coef3, ikb3, cache_kidx, page_table, layer):
    B, n_pages = page_table.shape
    G = min(PAGES_PER_STEP, n_pages)
    n_keys = n_pages * PAGE_SIZE + 1
    topk = min(TOPK_MAX, n_keys // 4)

    def page_spec(r):
        return pl.BlockSpec((None, None, PAGE_SIZE, IDX_DIM), lambda b, j, pt: (layer, pt[b, j * G + r], 0, 0))

    return pl.pallas_call(
        functools.partial(_dsa_select_body, topk=topk, nbits=n_keys.bit_length()),
        grid_spec=pltpu.PrefetchScalarGridSpec(
            num_scalar_prefetch=1,
            grid=(B, n_pages // G),
            in_specs=[pl.BlockSpec((None, IDX_HEADS, IDX_DIM), lambda b, j, pt: (b, 0, 0)),
                      pl.BlockSpec((None, IDX_HEADS, 1), lambda b, j, pt: (b, 0, 0)),
                      pl.BlockSpec((None, 1, IDX_DIM), lambda b, j, pt: (b, 0, 0))]
                     + [page_spec(r) for r in range(G)],
            out_specs=[pl.BlockSpec((None, WORDS_PER_PAGE, n_pages), lambda b, j, pt: (b, 0, 0)),
                       pl.BlockSpec((None, 1, 128), lambda b, j, pt: (b, 0, 0))],
            scratch_shapes=[pltpu.VMEM((n_pages, PAGE_SIZE), F32)],
        ),
        out_shape=[jax.ShapeDtypeStruct((B, WORDS_PER_PAGE, n_pages), I32),
                   jax.ShapeDtypeStruct((B, 1, 128), I32)],
        compiler_params=_cparams(("parallel", "arbitrary")),
        name="dsa_sample_select",
    )(page_table, iq3, coef3, ikb3, *([cache_kidx] * G))


def _row_copies(ck_ref, cv_ref, kbuf, vbuf, sem, layer, phys, slot, row):
    dst = pl.ds(pl.multiple_of(row * N_HEADS, N_HEADS), N_HEADS)
    return (pltpu.make_async_copy(ck_ref.at[layer, phys, slot], kbuf.at[dst, :], sem.at[0]),
            pltpu.make_async_copy(cv_ref.at[layer, phys, slot], vbuf.at[dst, :], sem.at[1]))


def _dsa_gather_body(pt_ref, ctz_ref, words_ref, meta_ref, q_ref, kn_ref, vn_ref, ck_ref, cv_ref, o_ref,
                     kbuf, vbuf, cnt_ref, sem, *, layer, cap):
    b = pl.program_id(0)
    n_words = words_ref.shape[-1] * WORDS_PER_PAGE
    want = jnp.minimum(meta_ref[0, META_COUNT], cap)
    kbuf[...] = jnp.zeros(kbuf.shape, F32)
    vbuf[...] = jnp.zeros(vbuf.shape, F32)
    cnt_ref[0] = 0

    def scan_word(carry):
        i, _ = carry
        p = lax.shift_right_logical(i, np.int32(WORDS_PER_PAGE.bit_length() - 1))
        w = i & np.int32(WORDS_PER_PAGE - 1)
        word = words_ref[w, p]

        @pl.when(word != 0)
        def _():
            phys = pt_ref[b, p]

            def next_bit(rest):
                low = rest & (-rest)
                bit = ctz_ref[lax.shift_right_logical(low * CTZ_MULTIPLIER, np.int32(27))]
                n = cnt_ref[0]

                @pl.when(n < want)
                def _():
                    for cp in _row_copies(ck_ref, cv_ref, kbuf, vbuf, sem, layer, phys, w * WORD_BITS + bit, n):
                        cp.start()
                    cnt_ref[0] = n + 1

                return rest ^ low

            lax.while_loop(lambda rest: rest != 0, next_bit, word)

        return i + 1, cnt_ref[0]

    _, cnt = lax.while_loop(lambda c: (c[0] < n_words) & (c[1] < want), scan_word, (np.int32(0), np.int32(0)))

    def wait_loop(n, carry):
        for cp in _row_copies(ck_ref, cv_ref, kbuf, vbuf, sem, layer, 0, 0, n):
            cp.wait()
        return carry

    lax.fori_loop(0, cnt, wait_loop, 0)

    q8 = q_ref[...]
    valid = lax.broadcasted_iota(I32, (N_HEADS, cap), 1) < cnt
    seln = meta_ref[0, META_NEW] > 0
    sn = jnp.sum(q8.astype(F32) * kn_ref[...].astype(F32), axis=1, keepdims=True)
    vn = vn_ref[...].astype(F32)
    head = lax.broadcasted_iota(I32, (N_HEADS, HEAD_DIM), 0)
    out = jnp.zeros((N_HEADS, HEAD_DIM), F32)
    for h in range(N_HEADS):
        kh = kbuf[pl.ds(h, cap, stride=N_HEADS), :].astype(BF16)
        vh = vbuf[pl.ds(h, cap, stride=N_HEADS), :].astype(BF16)
        s = _dot_nt(q8, kh)
        m = jnp.maximum(jnp.max(jnp.where(valid, s, NEG_BIG), axis=1, keepdims=True), jnp.where(seln, sn, NEG_BIG))
        p = jnp.where(valid, jnp.exp(s - m), 0.0)
        pn = jnp.where(seln, jnp.exp(sn - m), 0.0)
        l = jnp.sum(p, axis=1, keepdims=True) + pn
        o = (_dot(p.astype(BF16), vh) + pn.astype(BF16).astype(F32) * vn) / l
        out = jnp.where(head == h, o, out)
    o_ref[...] = out.astype(BF16)


def dsa_sample_gather(words, meta, q8, kn8, vn8, cache_k, cache_v, page_table, layer):
    B, n_pages = page_table.shape
    cap = min(TOPK_MAX, (n_pages * PAGE_SIZE + 1) // 4)
    head_rows = pl.BlockSpec((None, N_HEADS, HEAD_DIM), lambda b, pt, ctz: (b, 0, 0))
    return pl.pallas_call(
        functools.partial(_dsa_gather_body, layer=layer, cap=cap),
        grid_spec=pltpu.PrefetchScalarGridSpec(
            num_scalar_prefetch=2,
            grid=(B,),
            in_specs=[pl.BlockSpec((None, WORDS_PER_PAGE, n_pages), lambda b, pt, ctz: (b, 0, 0),
                                   memory_space=pltpu.SMEM),
                      pl.BlockSpec((None, 1, 128), lambda b, pt, ctz: (b, 0, 0), memory_space=pltpu.SMEM),
                      head_rows, head_rows, head_rows,
                      pl.BlockSpec(memory_space=pl.ANY), pl.BlockSpec(memory_space=pl.ANY)],
            out_specs=head_rows,
            scratch_shapes=[pltpu.VMEM((cap * N_HEADS, HEAD_DIM), F32), pltpu.VMEM((cap * N_HEADS, HEAD_DIM), F32),
                            pltpu.SMEM((1,), I32), pltpu.SemaphoreType.DMA((2,))],
        ),
        out_shape=jax.ShapeDtypeStruct((B, N_HEADS, HEAD_DIM), BF16),
        compiler_params=_cparams(("arbitrary",)),
        name="dsa_sample_gather",
    )(page_table, jnp.asarray(CTZ_TABLE), words, meta, q8, kn8, vn8, cache_k, cache_v)


def _merge_body(x_ref, hm_ref, att_ref, ga_ref, gb_ref, wa_ref, wb_ref, wo_ref, o_ref):
    br_a = _dot(hm_ref[...], wa_ref[...])
    br_b = _dot(att_ref[...], wb_ref[...])
    t = jax.nn.sigmoid(ga_ref[...].astype(F32)) * br_a + jax.nn.sigmoid(gb_ref[...].astype(F32)) * br_b
    o_ref[...] = x_ref[...] + _dot(t.astype(BF16), wo_ref[...])


def merge(x, hm, att, zg, wa_all, wb_all, wo_all, layer, *, tm):
    M, D = x.shape
    wspec = lambda k: pl.BlockSpec((None, k, D), lambda i: (layer, 0, 0))
    return pl.pallas_call(
        _merge_body,
        grid=(M // tm,),
        in_specs=[pl.BlockSpec((tm, D), lambda i: (i, 0)),
                  pl.BlockSpec((tm, M_V), lambda i: (i, 0)),
                  pl.BlockSpec((tm, A_W), lambda i: (i, 0)),
                  pl.BlockSpec((tm, D), lambda i: (i, C_GA // D)),
                  pl.BlockSpec((tm, D), lambda i: (i, C_GB // D)),
                  wspec(M_V), wspec(A_W), wspec(D)],
        out_specs=pl.BlockSpec((tm, D), lambda i: (i, 0)),
        out_shape=jax.ShapeDtypeStruct((M, D), F32),
        compiler_params=_cparams(("parallel",)),
        name="merge",
    )(x, hm, att, zg, zg, wa_all, wb_all, wo_all)


def relayout_w_in(w_in, b_in):
    def cols(a):
        mq, mk, mv, mo = a[..., 0:512], a[..., 512:1024], a[..., 1024:2048], a[..., 2048:3072]
        mi, mf = a[..., 3072:3076], a[..., 3076:3080]
        aq, ak, av = a[..., 3080:4104], a[..., 4104:5128], a[..., 5128:6152]
        iq, ik, iw = a[..., 6152:7176], a[..., 7176:7240], a[..., 7240:7256]
        ga, gb = a[..., 7256:9304], a[..., 9304:11352]
        pad = jnp.zeros(a.shape[:-1] + (NZ_A - C_MISC - L_MF - M_HEADS,), a.dtype)
        return (jnp.concatenate([ga, gb, mq, mk, mv, mo], axis=-1),
                jnp.concatenate([aq, ak, av, iq, ik, iw, mi, mf, pad], axis=-1))

    wg, wa = cols(w_in)
    bg, ba = cols(b_in)
    return wg.astype(BF16), wa.astype(BF16), bg[:, None, :], ba[:, None, :]


FFN_ROWS = 1024
FFN_COLS = 256
INPROJ_G_COLS = 1024
INPROJ_A_ROWS = 512
INPROJ_A_COLS = NZ_A // 2
POST_ROWS = 512
MLSTM_CHUNK = 256
SAMPLE_PAD = 16


def _mixer_block(x, layer, P, tabs, kv_rows, *, depth, tm, ta, tq, g_dtype, tab_blocks, transposed, mixers):
    zg = inproj(x, P["g_mix"], P["wg"], P["bg"], layer, tm=tm, tn=INPROJ_G_COLS, out_dtype=g_dtype)
    za = inproj(x, P["g_mix"], P["wa"], P["ba"], layer, tm=ta, tn=INPROJ_A_COLS, out_dtype=F32)
    post = qkpost(za, P["g_q"], P["g_k"], P["g_ik"], tabs, layer, kv_rows, depth=depth, tm=tq,
                  tab_blocks=tab_blocks, transposed=transposed)
    kv_rows, miscr = (post[1], post[3]), post[6]
    hm, att, state = mixers(zg, za, post)
    x = merge(x, hm, att, zg, P["w_a"], P["w_b"], P["w_o"], layer, tm=tq)
    return x, kv_rows, (miscr[:, :IDX_DIM],) + state


def kernel(x_prompt, x_sample, cache_k, cache_v, cache_kidx, state_C, state_n, state_m, page_table, norm_ffn1, w_ffn1_gu, w_ffn1_down, norm_mix, w_in, b_in, q_norm, k_norm, idx_k_norm, mlstm_norm, w_branch_a, w_branch_b, w_out, norm_ffn2, w_ffn2_gu, w_ffn2_down):
    B, S, D = x_prompt.shape
    BS = x_sample.shape[0]
    depth = w_in.shape[0]
    past = page_table.shape[1] * PAGE_SIZE

    wg, wa, bg, ba = relayout_w_in(w_in, b_in)
    pad128 = lambda g: jnp.pad(g, ((0, 0), (0, 128 - g.shape[-1])))[:, None, :]
    P = dict(g_f1=norm_ffn1[:, None, :], w_f1_gu=w_ffn1_gu, w_f1_dn=w_ffn1_down,
             g_mix=norm_mix[:, None, :], wg=wg, wa=wa, bg=bg, ba=ba,
             g_q=q_norm[:, None, :], g_k=k_norm[:, None, :], g_ik=pad128(idx_k_norm),
             g_m=mlstm_norm.reshape(depth, 1, M_V),
             w_a=w_branch_a.astype(BF16), w_b=w_branch_b.astype(BF16), w_o=w_out.astype(BF16),
             g_f2=norm_ffn2[:, None, :], w_f2_gu=w_ffn2_gu, w_f2_dn=w_ffn2_down)
    tabs_p = rope_tables(jnp.arange(S))
    tabs_s = rope_tables(jnp.full((BS,), past, jnp.int32))

    xp = x_prompt.reshape(B * S, D)
    xs = x_sample.reshape(BS, D)
    pad_rows = lambda a: jnp.pad(a, ((0, SAMPLE_PAD - BS), (0, 0)))
    tm = min(FFN_ROWS, B * S)
    ta = min(INPROJ_A_ROWS, B * S)
    tq = min(POST_ROWS, B * S)
    rows_p, rows_s = [], []
    kv_p = tuple(jnp.zeros((depth, B * S, N_HEADS, HEAD_DIM), F32) for _ in range(2))
    kv_s = tuple(jnp.zeros((depth, BS, N_HEADS, HEAD_DIM), F32) for _ in range(2))
    for layer in range(depth):
        def mix_prompt(zg, za, post):
            qt, _, kb, _, vt, iqt, miscr, misct = post
            hm, C, n, m = mlstm_prompt(zg, za, P["g_m"], layer, B=B, S=S, T=MLSTM_CHUNK)
            att = dsa_prompt(qt, iqt, misct, miscr, kb, vt, B=B, S=S)
            return hm, att, (C, n, m)

        def mix_sample(zg, za, post):
            qb, _, kb, _, vb, iqb, miscr = post
            hm, C, n, m = mlstm_step(zg, za, P["g_m"], state_C, state_n, state_m, layer)
            iq3 = iqb.reshape(BS, IDX_HEADS, IDX_DIM)
            coef3 = miscr[:, L_IW:L_IW + IDX_HEADS].reshape(BS, IDX_HEADS, 1)
            ikb3 = miscr[:, :IDX_DIM].astype(BF16).reshape(BS, 1, IDX_DIM)
            words, meta = dsa_sample_select(iq3, coef3, ikb3, cache_kidx, page_table, layer)
            heads = lambda a: a.reshape(BS, N_HEADS, HEAD_DIM)
            att = dsa_sample_gather(words, meta, heads(qb), heads(kb), heads(vb), cache_k, cache_v, page_table, layer)
            return hm, att.reshape(BS, A_W), (C, n, m)

        xp, xs16 = ffn(xp, pad_rows(xs), P["g_f1"], P["w_f1_gu"], P["w_f1_dn"], layer, tm=tm, tf=FFN_COLS)
        xp, kv_p, rp = _mixer_block(xp, layer, P, tabs_p, kv_p, depth=depth, tm=tm, ta=ta, tq=tq, g_dtype=BF16,
                                    tab_blocks=S // tq, transposed=True, mixers=mix_prompt)
        xs, kv_s, rs = _mixer_block(xs16[:BS], layer, P, tabs_s, kv_s, depth=depth, tm=BS, ta=BS, tq=BS, g_dtype=F32,
                                    tab_blocks=1, transposed=False, mixers=mix_sample)
        xp, xs16 = ffn(xp, pad_rows(xs), P["g_f2"], P["w_f2_gu"], P["w_f2_dn"], layer, tm=tm, tf=FFN_COLS)
        xs = xs16[:BS]
        rows_p.append(rp)
        rows_s.append(rs)

    def stack(rows, idx, shape):
        return jnp.stack([r[idx] for r in rows]).reshape((depth,) + shape)

    return (xp.reshape(B, S, D), xs.reshape(BS, 1, D),
            kv_p[0].reshape(depth, B, S, N_HEADS, HEAD_DIM), kv_p[1].reshape(depth, B, S, N_HEADS, HEAD_DIM),
            stack(rows_p, 0, (B, S, IDX_DIM)),
            stack(rows_p, 1, (B, M_HEADS, M_QK_DIM, M_V_DIM)), stack(rows_p, 2, (B, M_HEADS, M_QK_DIM)),
            stack(rows_p, 3, (B, M_HEADS)),
            kv_s[0].reshape(depth, BS, 1, N_HEADS, HEAD_DIM), kv_s[1].reshape(depth, BS, 1, N_HEADS, HEAD_DIM),
            stack(rows_s, 0, (BS, 1, IDX_DIM)),
            stack(rows_s, 1, (BS, M_HEADS, M_QK_DIM, M_V_DIM)), stack(rows_s, 2, (BS, M_HEADS, M_QK_DIM)),
            stack(rows_s, 3, (BS, M_HEADS)))
```

```python
import functools

import jax
import jax.numpy as jnp
import numpy as np
from jax import lax
from jax.experimental import pallas as pl
from jax.experimental.pallas import tpu as pltpu

F32 = jnp.float32
BF16 = jnp.bfloat16
I32 = jnp.int32

EPS = 1e-6
ROPE_THETA = 500000.0
PAGE_SIZE = 128
TOPK_MAX = 256

M_HEADS = 4
M_QK_DIM = 128
M_V_DIM = 256
N_HEADS = 8
HEAD_DIM = 128
IDX_HEADS = 16
IDX_DIM = 64
D_MODEL = 2048

M_QK = M_HEADS * M_QK_DIM
M_V = M_HEADS * M_V_DIM
A_W = N_HEADS * HEAD_DIM
IDX_QW = IDX_HEADS * IDX_DIM

C_GA = 0
C_GB = C_GA + D_MODEL
C_MQ = C_GB + D_MODEL
C_MK = C_MQ + M_QK
C_MV = C_MK + M_QK
C_MO = C_MV + M_V
NZ_G = C_MO + M_V
C_AQ = 0
C_AK = C_AQ + A_W
C_AV = C_AK + A_W
C_IQ = C_AV + A_W
C_MISC = C_IQ + IDX_QW
L_IK = 0
L_IW = IDX_DIM
L_MI = L_IW + IDX_HEADS
L_MF = L_MI + M_HEADS
MISC_W = 128
RELAYOUT_W = 512
NZ_A = C_MISC + RELAYOUT_W

VMEM_LIMIT = 58 * 1024 * 1024

NEG_BIG = -1e30
MAX_INIT = -1e20
KEY_TILE = 256


def _cparams(sem, vmem=VMEM_LIMIT):
    return pltpu.CompilerParams(dimension_semantics=sem, vmem_limit_bytes=vmem)


def _rms(x, g):
    return x * lax.rsqrt(jnp.mean(x * x, axis=-1, keepdims=True) + EPS) * g


def _dot(a, b):
    return jnp.dot(a, b, preferred_element_type=F32)


def _dot_nt(a, b):
    return lax.dot_general(a, b, (((1,), (1,)), ((), ())), preferred_element_type=F32)


def _log_sigmoid(x):
    return jnp.minimum(x, 0.0) - jnp.log1p(jnp.exp(-jnp.abs(x)))


def _ffn_body(x_ref, xs_ref, g_ref, wg_ref, wu_ref, wd_ref, o_ref, os_ref, h_ref):
    i, j = pl.program_id(0), pl.program_id(1)
    tm = x_ref.shape[0]

    @pl.when(j == 0)
    def _():
        x = x_ref[...]
        h_ref[:tm, :] = _rms(x, g_ref[...]).astype(BF16)
        o_ref[...] = x

    @pl.when((i == 0) & (j == 0))
    def _():
        xs = xs_ref[...]
        h_ref[tm:, :] = _rms(xs, g_ref[...]).astype(BF16)
        os_ref[...] = xs

    h = h_ref[...]
    a = _dot(h, wg_ref[...].astype(BF16))
    u = _dot(h, wu_ref[...].astype(BF16))
    p = (0.5 * a) * jax.nn.sigmoid(a) * u
    r = _dot(p.astype(BF16), wd_ref[...].astype(BF16))
    o_ref[...] += r[:tm]

    @pl.when(i == 0)
    def _():
        os_ref[...] += r[tm:]


def ffn(x, xs, g_all, wgu_all, wdn_all, layer, *, tm, tf):
    M, D = x.shape
    MS = xs.shape[0]
    F = wdn_all.shape[1]
    nf = F // tf
    return pl.pallas_call(
        _ffn_body,
        grid=(M // tm, nf),
        in_specs=[
            pl.BlockSpec((tm, D), lambda i, j: (i, 0)),
            pl.BlockSpec((MS, D), lambda i, j: (0, 0)),
            pl.BlockSpec((None, 1, D), lambda i, j: (layer, 0, 0)),
            pl.BlockSpec((None, D, tf), lambda i, j: (layer, 0, j)),
            pl.BlockSpec((None, D, tf), lambda i, j: (layer, 0, j + nf)),
            pl.BlockSpec((None, tf, D), lambda i, j: (layer, j, 0)),
        ],
        out_specs=[pl.BlockSpec((tm, D), lambda i, j: (i, 0)),
                   pl.BlockSpec((MS, D), lambda i, j: (0, 0))],
        out_shape=[jax.ShapeDtypeStruct((M, D), F32), jax.ShapeDtypeStruct((MS, D), F32)],
        scratch_shapes=[pltpu.VMEM((tm + MS, D), BF16)],
        compiler_params=_cparams(("arbitrary", "arbitrary")),
        name="ffn",
    )(x, xs, g_all, wgu_all, wgu_all, wdn_all)


def _inproj_body(x_ref, g_ref, w_ref, b_ref, z_ref, h_ref):
    @pl.when(pl.program_id(1) == 0)
    def _():
        h_ref[...] = _rms(x_ref[...], g_ref[...]).astype(BF16)

    z_ref[...] = (_dot(h_ref[...], w_ref[...]) + b_ref[...]).astype(z_ref.dtype)


def inproj(x, g_all, wr_all, br_all, layer, *, tm, tn, out_dtype):
    M, D = x.shape
    NZ = wr_all.shape[-1]
    return pl.pallas_call(
        _inproj_body,
        grid=(M // tm, NZ // tn),
        in_specs=[
            pl.BlockSpec((tm, D), lambda i, j: (i, 0)),
            pl.BlockSpec((None, 1, D), lambda i, j: (layer, 0, 0)),
            pl.BlockSpec((None, D, tn), lambda i, j: (layer, 0, j)),
            pl.BlockSpec((None, 1, tn), lambda i, j: (layer, 0, j)),
        ],
        out_specs=pl.BlockSpec((tm, tn), lambda i, j: (i, j)),
        out_shape=jax.ShapeDtypeStruct((M, NZ), out_dtype),
        scratch_shapes=[pltpu.VMEM((tm, D), BF16)],
        compiler_params=_cparams(("parallel", "arbitrary")),
        name="inproj",
    )(x, g_all, wr_all, br_all)


def _rope(x, c, sa, sb, shift):
    n = x.shape[-1]
    return x * c + pltpu.roll(x, n - shift, 1) * sa + pltpu.roll(x, shift, 1) * sb


def _qkpost_body(*refs, transposed, n_carried):
    (aq_ref, ak_ref, av_ref, iq_ref, misc_ref, gq_ref, gk_ref, gik_ref,
     c128_ref, sa128_ref, sb128_ref, c64_ref, sa64_ref, sb64_ref) = refs[:14]
    q_ref, kf_ref, kb_ref, vf_ref, v_ref, iqo_ref, miscr_ref, *misct_ref = refs[14 + n_carried:]
    c128, sa128, sb128 = c128_ref[...], sa128_ref[...], sb128_ref[...]
    c64, sa64, sb64 = c64_ref[...], sa64_ref[...], sb64_ref[...]
    gq, gk = gq_ref[...], gk_ref[...]
    half128 = HEAD_DIM // 8
    half64 = IDX_DIM // 8
    tm = aq_ref.shape[0]
    for h in range(N_HEADS):
        sl = slice(h * HEAD_DIM, (h + 1) * HEAD_DIM)
        q = _rope(_rms(aq_ref[:, sl], gq), c128, sa128, sb128, half128) * (HEAD_DIM ** -0.5)
        k = _rope(_rms(ak_ref[:, sl], gk), c128, sa128, sb128, half128)
        kf_ref[:, h, :] = k
        kb_ref[:, sl] = k.astype(BF16)
        vf_ref[:, h, :] = av_ref[:, sl]
        if transposed:
            q_ref[sl, :] = q.T.astype(BF16)
            for t in range(tm // KEY_TILE):
                v_ref[t, sl, :] = av_ref[t * KEY_TILE:(t + 1) * KEY_TILE, sl].T.astype(BF16)
        else:
            q_ref[:, sl] = q.astype(BF16)
            v_ref[:, sl] = av_ref[:, sl].astype(BF16)
    for h in range(IDX_QW // 128):
        sl = slice(h * 128, (h + 1) * 128)
        iq = _rope(iq_ref[:, sl], c64, sa64, sb64, half64)
        if transposed:
            iqo_ref[sl, :] = iq.T.astype(BF16)
        else:
            iqo_ref[:, sl] = iq.astype(BF16)
    misc = misc_ref[...]
    lane = lax.broadcasted_iota(I32, misc.shape, 1)
    is_ik = lane < IDX_DIM
    ikv = jnp.where(is_ik, misc, 0.0)
    ms = jnp.sum(ikv * ikv, axis=-1, keepdims=True) * (1.0 / IDX_DIM)
    ikn = ikv * lax.rsqrt(ms + EPS) * gik_ref[...]
    ikr = _rope(ikn, c64, sa64, sb64, half64)
    is_iw = (lane >= L_IW) & (lane < L_IW + IDX_HEADS)
    iw_scale = (IDX_HEADS ** -0.5) * (IDX_DIM ** -0.5)
    miscr = jnp.where(is_ik, ikr, jnp.where(is_iw, misc * iw_scale, misc))
    miscr_ref[...] = miscr
    if transposed:
        misct_ref[0][...] = miscr.T


def qkpost(z, gq_all, gk_all, gik_all, tabs, layer, carried, *, depth, tm, tab_blocks, transposed):
    M = z.shape[0]
    zspec = lambda w, c: pl.BlockSpec((tm, w), lambda i: (i, c // w))
    tspec = pl.BlockSpec((tm, 128), lambda i: (i % tab_blocks, 0))
    gspec = pl.BlockSpec((None, 1, 128), lambda i: (layer, 0, 0))
    row = lambda w: pl.BlockSpec((tm, w), lambda i: (i, 0))
    col = lambda w: pl.BlockSpec((w, tm), lambda i: (0, i))
    rows_spec = pl.BlockSpec((None, tm, N_HEADS, HEAD_DIM), lambda i: (layer, i, 0, 0))
    sds = jax.ShapeDtypeStruct
    rows_shape = sds((depth, M, N_HEADS, HEAD_DIM), F32)
    if transposed:
        nt = tm // KEY_TILE
        out_specs = [col(A_W), rows_spec, row(A_W), rows_spec, pl.BlockSpec((nt, A_W, KEY_TILE), lambda i: (i, 0, 0)),
                     col(IDX_QW), row(MISC_W), col(MISC_W)]
        out_shape = [sds((A_W, M), BF16), rows_shape, sds((M, A_W), BF16), rows_shape,
                     sds((M // KEY_TILE, A_W, KEY_TILE), BF16), sds((IDX_QW, M), BF16), sds((M, MISC_W), F32),
                     sds((MISC_W, M), F32)]
    else:
        out_specs = [row(A_W), rows_spec, row(A_W), rows_spec, row(A_W), row(IDX_QW), row(MISC_W)]
        out_shape = [sds((M, A_W), BF16), rows_shape, sds((M, A_W), BF16), rows_shape,
                     sds((M, A_W), BF16), sds((M, IDX_QW), BF16), sds((M, MISC_W), F32)]
    n_in = 14
    return pl.pallas_call(
        functools.partial(_qkpost_body, transposed=transposed, n_carried=len(carried)),
        grid=(M // tm,),
        in_specs=[zspec(A_W, C_AQ), zspec(A_W, C_AK), zspec(A_W, C_AV), zspec(IDX_QW, C_IQ),
                  zspec(MISC_W, C_MISC), gspec, gspec, gspec] + [tspec] * 6
                 + [pl.BlockSpec(memory_space=pl.ANY)] * len(carried),
        out_specs=out_specs,
        out_shape=out_shape,
        input_output_aliases={n_in + c: o for c, o in zip(range(len(carried)), (1, 3))},
        compiler_params=_cparams(("parallel",)),
        name="qkpost",
    )(z, z, z, z, z, gq_all, gk_all, gik_all, *tabs, *carried)


def rope_tables(pos):
    pos = pos.astype(F32)[:, None]

    def tables(width):
        rd = width // 4
        half = rd // 2
        inv = ROPE_THETA ** (-jnp.arange(half, dtype=F32) * (2.0 / rd))
        lane = np.arange(128) % width
        ang = pos * inv[None, :][:, lane % half]
        cos, sin = jnp.cos(ang), jnp.sin(ang)
        lo = jnp.asarray(lane < half)[None, :]
        hi = jnp.asarray((lane >= half) & (lane < rd))[None, :]
        c = jnp.where(lo | hi, cos, 1.0)
        sa = jnp.where(lo, -sin, 0.0)
        sb = jnp.where(hi, sin, 0.0)
        return c, sa, sb

    return tables(HEAD_DIM) + tables(IDX_DIM)


def _split3(x):
    hi = x.astype(BF16)
    r1 = x - hi.astype(F32)
    mid = r1.astype(BF16)
    lo = (r1 - mid.astype(F32)).astype(BF16)
    return hi, mid, lo


def _mlstm_body(zq_ref, zk_ref, zv_ref, zo_ref, misc_ref, gm_ref, s0_ref, hm_ref, sout_ref, s_ref, m_ref):
    c = pl.program_id(1)
    T = zq_ref.shape[0]
    EW = s_ref.shape[-1]

    @pl.when(c == 0)
    def _():
        s_ref[...] = s0_ref[...]
        for h in range(M_HEADS):
            m_ref[h:h + 1, :] = jnp.broadcast_to(s0_ref[h, 0:1, M_V_DIM + 1:M_V_DIM + 2], (1, 128))

    misc = misc_ref[...]
    misc_t = misc.T
    gates_t = misc_t[L_MI:L_MI + 8, :]
    lf_rows = _log_sigmoid(gates_t)
    lf_cols = _log_sigmoid(misc)
    ti = lax.broadcasted_iota(I32, (T, T), 0)
    si = lax.broadcasted_iota(I32, (T, T), 1)
    causal = si <= ti
    ltri = jnp.where(causal, 1.0, 0.0).astype(BF16)
    utri = jnp.where(ti <= si, 1.0, 0.0).astype(BF16)
    b_cols = sum(_dot(ltri, p) for p in _split3(lf_cols))
    b_rows = sum(_dot(p, utri) for p in _split3(lf_rows))
    ones_col = jnp.where(lax.broadcasted_iota(I32, (T, EW - M_V_DIM), 1) == 0, 1.0, 0.0).astype(BF16)

    for h in range(M_HEADS):
        q = (zq_ref[:, h * M_QK_DIM:(h + 1) * M_QK_DIM].astype(F32) * (M_QK_DIM ** -0.5)).astype(BF16)
        k = zk_ref[:, h * M_QK_DIM:(h + 1) * M_QK_DIM].astype(F32)
        ve = jnp.concatenate([zv_ref[:, h * M_V_DIM:(h + 1) * M_V_DIM].astype(BF16), ones_col], axis=1)
        bcol = b_cols[:, L_MF + h:L_MF + h + 1]
        brow = b_rows[M_HEADS + h:M_HEADS + h + 1, :]
        r = gates_t[h:h + 1, :] - brow
        m0 = m_ref[h:h + 1, 0:1]
        d = jnp.where(causal, bcol + r, -jnp.inf)
        inter = bcol + m0
        m = jnp.maximum(inter, jnp.max(d, axis=1, keepdims=True))
        w = (_dot_nt(q, k.astype(BF16)) * jnp.exp(d - m)).astype(BF16)
        c_inter = jnp.exp(inter - m)
        s0 = s_ref[h]
        nd = c_inter * _dot(q, s0.astype(BF16)) + _dot(w, ve)
        hh = nd[:, :M_V_DIM] / jnp.maximum(jnp.abs(nd[:, M_V_DIM:M_V_DIM + 1]), jnp.exp(-m))
        sl = slice(h * M_V_DIM, (h + 1) * M_V_DIM)
        hm_ref[:, sl] = (_rms(hh, gm_ref[:, sl]) * jax.nn.sigmoid(zo_ref[:, sl].astype(F32))).astype(BF16)

        b_end = brow[:, T - 1:T]
        g = b_end + r
        m_new = jnp.maximum(b_end + m0, jnp.max(g, axis=1, keepdims=True))
        a = jnp.exp(g - m_new)
        c_old = jnp.exp(b_end + m0 - m_new)
        s_ref[h] = c_old * s0 + _dot((k.T * a).astype(BF16), ve)
        m_ref[h:h + 1, :] = jnp.broadcast_to(m_new, (1, 128))

    @pl.when(c == pl.num_programs(1) - 1)
    def _():
        ri = lax.broadcasted_iota(I32, s_ref.shape[1:], 0)
        ci = lax.broadcasted_iota(I32, s_ref.shape[1:], 1)
        for h in range(M_HEADS):
            sout_ref[h] = jnp.where((ri == 0) & (ci == M_V_DIM + 1), m_ref[h:h + 1, 0:1], s_ref[h])


def mlstm_prompt(zg, za, gm_all, layer, *, B, S, T):
    EW = M_V_DIM + 128
    nc = S // T
    s0 = jnp.zeros((M_HEADS, M_QK_DIM, EW), F32)
    zspec = lambda w, c0: pl.BlockSpec((T, w), lambda b, c: (b * nc + c, c0 // w))
    hm, sout = pl.pallas_call(
        _mlstm_body,
        grid=(B, nc),
        in_specs=[zspec(M_QK, C_MQ), zspec(M_QK, C_MK), zspec(M_V, C_MV), zspec(M_V, C_MO), zspec(MISC_W, C_MISC),
                  pl.BlockSpec((None, 1, M_V), lambda b, c: (layer, 0, 0)),
                  pl.BlockSpec((M_HEADS, M_QK_DIM, EW), lambda b, c: (0, 0, 0))],
        out_specs=[pl.BlockSpec((T, M_V), lambda b, c: (b * nc + c, 0)),
                   pl.BlockSpec((None, M_HEADS, M_QK_DIM, EW), lambda b, c: (b, 0, 0, 0))],
        out_shape=[jax.ShapeDtypeStruct((B * S, M_V), BF16),
                   jax.ShapeDtypeStruct((B, M_HEADS, M_QK_DIM, EW), F32)],
        scratch_shapes=[pltpu.VMEM((M_HEADS, M_QK_DIM, EW), F32), pltpu.VMEM((8, 128), F32)],
        compiler_params=_cparams(("parallel", "arbitrary")),
        name="mlstm_prompt",
    )(zg, zg, zg, zg, za, gm_all, s0)
    return hm, sout[..., :M_V_DIM], sout[..., M_V_DIM], sout[:, :, 0, M_V_DIM + 1]


def _mlstm_step_body(zq_ref, zk_ref, zv_ref, zo_ref, misc_ref, gm_ref, c_ref, n_ref, m_ref,
                     hm_ref, cout_ref, nout_ref, mout_ref):
    misc = misc_ref[...]
    lf_all = _log_sigmoid(misc)
    eye = lax.broadcasted_iota(I32, (M_QK_DIM, M_QK_DIM), 0) == lax.broadcasted_iota(I32, (M_QK_DIM, M_QK_DIM), 1)
    for h in range(M_HEADS):
        q = zq_ref[:, h * M_QK_DIM:(h + 1) * M_QK_DIM].astype(F32) * (M_QK_DIM ** -0.5)
        k = zk_ref[:, h * M_QK_DIM:(h + 1) * M_QK_DIM].astype(F32)
        sl = slice(h * M_V_DIM, (h + 1) * M_V_DIM)
        v = zv_ref[:, sl].astype(F32)
        li = misc[:, L_MI + h:L_MI + h + 1]
        lf = lf_all[:, L_MF + h:L_MF + h + 1]
        s0 = c_ref[h]
        n0 = n_ref[h:h + 1, :]
        m0 = m_ref[:, h:h + 1]
        m = jnp.maximum(lf + m0, li)
        w = jnp.sum(q * k, axis=1, keepdims=True) * jnp.exp(li - m)
        c_inter = jnp.exp(lf + m0 - m)
        q8 = jnp.broadcast_to(q, (8, M_QK_DIM)).astype(BF16)
        qs = _dot(q8, s0.astype(BF16))[0:1, :]
        num = c_inter * qs + w * v
        den = c_inter * jnp.sum(q * n0, axis=1, keepdims=True) + w
        hh = num / jnp.maximum(jnp.abs(den), jnp.exp(-m))
        hm_ref[:, sl] = (_rms(hh, gm_ref[:, sl]) * jax.nn.sigmoid(zo_ref[:, sl].astype(F32))).astype(BF16)
        a = jnp.exp(li - m)
        kcol = jnp.sum(jnp.where(eye, jnp.broadcast_to(k, (M_QK_DIM, M_QK_DIM)), 0.0), axis=1, keepdims=True)
        cout_ref[h] = c_inter * s0 + (a * kcol) * v
        nout_ref[h:h + 1, :] = c_inter * n0 + a * k
        mout_ref[:, h:h + 1] = m


def mlstm_step(zg, za, gm_all, state_c, state_n, state_m, layer):
    B = zg.shape[0]
    zg3 = zg.reshape(B, 1, NZ_G)
    za3 = za.reshape(B, 1, NZ_A)
    m4 = state_m.reshape(state_m.shape[0], B, 1, M_HEADS)
    zspec = lambda w, c0: pl.BlockSpec((None, 1, w), lambda b: (b, 0, c0 // w))
    hm, cout, nout, mout = pl.pallas_call(
        _mlstm_step_body,
        grid=(B,),
        in_specs=[zspec(M_QK, C_MQ), zspec(M_QK, C_MK), zspec(M_V, C_MV), zspec(M_V, C_MO), zspec(MISC_W, C_MISC),
                  pl.BlockSpec((None, 1, M_V), lambda b: (layer, 0, 0)),
                  pl.BlockSpec((None, None, M_HEADS, M_QK_DIM, M_V_DIM), lambda b: (layer, b, 0, 0, 0)),
                  pl.BlockSpec((None, None, M_HEADS, M_QK_DIM), lambda b: (layer, b, 0, 0)),
                  pl.BlockSpec((None, None, 1, M_HEADS), lambda b: (layer, b, 0, 0))],
        out_specs=[pl.BlockSpec((None, 1, M_V), lambda b: (b, 0, 0)),
                   pl.BlockSpec((None, M_HEADS, M_QK_DIM, M_V_DIM), lambda b: (b, 0, 0, 0)),
                   pl.BlockSpec((None, M_HEADS, M_QK_DIM), lambda b: (b, 0, 0)),
                   pl.BlockSpec((None, 1, M_HEADS), lambda b: (b, 0, 0))],
        out_shape=[jax.ShapeDtypeStruct((B, 1, M_V), BF16),
                   jax.ShapeDtypeStruct((B, M_HEADS, M_QK_DIM, M_V_DIM), F32),
                   jax.ShapeDtypeStruct((B, M_HEADS, M_QK_DIM), F32),
                   jax.ShapeDtypeStruct((B, 1, M_HEADS), F32)],
        compiler_params=_cparams(("parallel",)),
        name="mlstm_step",
    )(zg3, zg3, zg3, zg3, za3, gm_all, state_c, state_n, m4)
    return hm.reshape(B, M_V), cout, nout, mout.reshape(B, M_HEADS)


_INT_MIN = np.int32(-2 ** 31)
_KEY_NEG_INF = np.int32(np.array(0xFF800000, np.uint32).view(np.int32) ^ np.int32(0x7FFFFFFF))


def _sort_key(x):
    bits = pltpu.bitcast(x, I32)
    return jnp.where(bits < 0, bits ^ np.int32(0x7FFFFFFF), bits)


def _topk_select(count, topk, nbits, shape):
    def bit_step(b, theta_u):
        cand_u = theta_u | lax.shift_left(np.int32(1), np.int32(31) - b)
        cand_s = cand_u ^ _INT_MIN
        cnt = count(lambda key, idx: jnp.where(key >= cand_s, 1.0, 0.0))
        return jnp.where(cnt >= float(topk), cand_u, theta_u)

    theta = lax.fori_loop(0, 32, bit_step, jnp.zeros(shape, I32)) ^ _INT_MIN
    room = float(topk) - count(lambda key, idx: jnp.where(key > theta, 1.0, 0.0))

    def tie_step(b, jlim):
        cand = jlim | lax.shift_left(np.int32(1), np.int32(nbits - 1) - b)
        f = count(lambda key, idx: jnp.where(key == theta, jnp.where(idx < cand, 1.0, 0.0), 0.0))
        return jnp.where(f <= room, cand, jlim)

    return theta, room, tie_step


def _dsa_prompt_body(qt_ref, iqt_ref, misct_ref, k_ref, vt_ref, misck_ref, o_ref,
                     key_ref, jlim_ref, m_ref, l_ref, alpha_ref, acc_ref, bias_ref, s_ref, p_ref, *, topk, nbits):
    i = pl.program_id(1)
    KT, QB = key_ref.shape[1:]
    nt = i + 1
    kloc = lax.broadcasted_iota(I32, (KT, QB), 0)
    qpos = i * QB + lax.broadcasted_iota(I32, (KT, QB), 1)
    coef = misct_ref[L_IW:L_IW + IDX_HEADS, :]

    def score_tile(j, carry):
        off = pl.multiple_of(j * KT, KT)
        ik = misck_ref[pl.ds(off, KT), :][:, :IDX_DIM].astype(BF16)
        sc = jnp.zeros((KT, QB), F32)
        for h in range(IDX_HEADS):
            d = _dot(ik, iqt_ref[h * IDX_DIM:(h + 1) * IDX_DIM, :])
            sc = sc + jnp.maximum(d, 0.0) * coef[h:h + 1, :]
        sc = jnp.where(j * KT + kloc <= qpos, sc, -jnp.inf)
        key_ref[j] = _sort_key(sc)
        return carry

    lax.fori_loop(0, nt, score_tile, 0)

    def count(f):
        def body(j, cnt8):
            return cnt8 + jnp.sum(f(key_ref[j], j * KT + kloc).reshape(KT // 8, 8, QB), axis=0)

        return jnp.sum(lax.fori_loop(0, nt, body, jnp.zeros((8, QB), F32)), axis=0, keepdims=True)

    theta, room, tie_step = _topk_select(count, topk, nbits, (1, QB))
    n_tie = count(lambda key, idx: jnp.where(key == theta, 1.0, 0.0))
    jlim_ref[...] = jnp.full((1, QB), 2 ** nbits, I32)
    crowded = jnp.max(jnp.where((n_tie > room) & (theta > _KEY_NEG_INF), 1.0, 0.0))

    @pl.when(crowded > 0.0)
    def _():
        jlim_ref[...] = lax.fori_loop(0, nbits, tie_step, jnp.zeros((1, QB), I32))

    jlim = jlim_ref[...]

    def finalize(j, carry):
        key = key_ref[j]
        sel = jnp.where(key > theta, 1, jnp.where(key == theta, jnp.where(j * KT + kloc < jlim, 1, 0), 0))
        key_ref[j] = jnp.where(key > _KEY_NEG_INF, sel, 0)
        return carry

    lax.fori_loop(0, nt, finalize, 0)

    m_ref[...] = jnp.full(m_ref.shape, MAX_INIT, F32)
    l_ref[...] = jnp.zeros(l_ref.shape, F32)
    acc_ref[...] = jnp.zeros(acc_ref.shape, F32)

    def att_tile(j, carry):
        off = pl.multiple_of(j * KT, KT)
        bias_ref[...] = jnp.where(key_ref[j] > 0, 0.0, NEG_BIG)
        for h in range(N_HEADS):
            sl = slice(h * HEAD_DIM, (h + 1) * HEAD_DIM)
            s_ref[h] = _dot(k_ref[pl.ds(off, KT), sl], qt_ref[sl, :]) + bias_ref[...]
        for h in range(N_HEADS):
            m_old = m_ref[h:h + 1, :]
            m_new = jnp.maximum(m_old, jnp.max(s_ref[h], axis=0, keepdims=True))
            p = jnp.exp(s_ref[h] - m_new)
            alpha = jnp.exp(m_old - m_new)
            l_ref[h:h + 1, :] = alpha * l_ref[h:h + 1, :] + jnp.sum(p, axis=0, keepdims=True)
            p_ref[h] = p.astype(BF16)
            alpha_ref[h:h + 1, :] = alpha
            m_ref[h:h + 1, :] = m_new
        for h in range(N_HEADS):
            sl = slice(h * HEAD_DIM, (h + 1) * HEAD_DIM)
            acc_ref[sl, :] = alpha_ref[h:h + 1, :] * acc_ref[sl, :] + _dot(vt_ref[j, sl, :], p_ref[h])
        return carry

    lax.fori_loop(0, nt, att_tile, 0)
    for h in range(N_HEADS):
        sl = slice(h * HEAD_DIM, (h + 1) * HEAD_DIM)
        o_ref[:, sl] = (acc_ref[sl, :] / l_ref[h:h + 1, :]).T.astype(BF16)


def dsa_prompt(qt, iqt, misct, miscr, kb, vt, *, B, S):
    QB = KEY_TILE
    topk = min(TOPK_MAX, S // 4)
    nq = S // QB
    col = lambda w: pl.BlockSpec((w, QB), lambda b, i: (0, b * nq + i))
    return pl.pallas_call(
        functools.partial(_dsa_prompt_body, topk=topk, nbits=S.bit_length()),
        grid=(B, nq),
        in_specs=[col(A_W), col(IDX_QW), col(MISC_W),
                  pl.BlockSpec((S, A_W), lambda b, i: (b, 0)),
                  pl.BlockSpec((nq, A_W, KEY_TILE), lambda b, i: (b, 0, 0)),
                  pl.BlockSpec((S, MISC_W), lambda b, i: (b, 0))],
        out_specs=pl.BlockSpec((QB, A_W), lambda b, i: (b * nq + i, 0)),
        out_shape=jax.ShapeDtypeStruct((B * S, A_W), BF16),
        scratch_shapes=[pltpu.VMEM((nq, KEY_TILE, QB), I32), pltpu.VMEM((1, QB), I32),
                        pltpu.VMEM((N_HEADS, QB), F32), pltpu.VMEM((N_HEADS, QB), F32), pltpu.VMEM((N_HEADS, QB), F32),
                        pltpu.VMEM((A_W, QB), F32), pltpu.VMEM((KEY_TILE, QB), F32),
                        pltpu.VMEM((N_HEADS, KEY_TILE, QB), F32), pltpu.VMEM((N_HEADS, KEY_TILE, QB), BF16)],
        compiler_params=_cparams(("parallel", "arbitrary")),
        name="dsa_prompt",
    )(qt, iqt, misct, kb, vt, miscr)


PAGES_PER_STEP = 32
WORD_BITS = 32
HALF_BITS = WORD_BITS // 2
WORDS_PER_PAGE = PAGE_SIZE // WORD_BITS
CTZ_MULTIPLIER = np.int32(0x077CB531)
CTZ_TABLE = np.array([0, 1, 28, 2, 29, 14, 24, 3, 30, 22, 20, 15, 25, 17, 4, 8,
                      31, 27, 13, 23, 21, 19, 16, 7, 26, 12, 18, 6, 11, 5, 10, 9], np.int32)
META_COUNT = 0
META_NEW = 1


def _dsa_select_body(pt_ref, iq_ref, coef_ref, ikn_ref, *rest, topk, nbits):
    page_refs = rest[:-3]
    words_ref, meta_ref, sc_ref = rest[-3:]
    G = len(page_refs)
    j = pl.program_id(1)
    iq = iq_ref[...]
    coef = coef_ref[...]
    for r, kref in enumerate(page_refs):
        d = _dot_nt(iq, kref[...].astype(BF16))
        sc_ref[pl.ds(j * G + r, 1), :] = jnp.sum(jnp.maximum(d, 0.0) * coef, axis=0, keepdims=True)

    @pl.when(j == pl.num_programs(1) - 1)
    def _():
        n_pages = sc_ref.shape[0]
        dn = jnp.sum(iq.astype(F32) * ikn_ref[...].astype(F32), axis=1, keepdims=True)
        key_new = _sort_key(jnp.sum(jnp.maximum(dn, 0.0) * coef, axis=0, keepdims=True))
        key = _sort_key(sc_ref[...])
        idx = (lax.broadcasted_iota(I32, key.shape, 0) * PAGE_SIZE + lax.broadcasted_iota(I32, key.shape, 1))
        idx_new = jnp.full((1, 1), n_pages * PAGE_SIZE, I32)

        def count(f):
            past = jnp.sum(jnp.sum(f(key, idx), axis=1, keepdims=True), axis=0, keepdims=True)
            return past + f(key_new, idx_new)

        theta, room, tie_step = _topk_select(count, topk, nbits, (1, 1))
        jlim = lax.fori_loop(0, nbits, tie_step, jnp.zeros((1, 1), I32))
        pick = lambda k, ix: jnp.where(k > theta, 1.0, jnp.where(k == theta, jnp.where(ix < jlim, 1.0, 0.0), 0.0))
        slot = lax.broadcasted_iota(I32, (2 * WORDS_PER_PAGE, PAGE_SIZE), 1)
        row = lax.broadcasted_iota(I32, (2 * WORDS_PER_PAGE, PAGE_SIZE), 0)
        half = jnp.where(row < WORDS_PER_PAGE, 2 * row, 2 * (row - WORDS_PER_PAGE) + 1)
        in_half = lax.shift_right_logical(slot, np.int32(HALF_BITS.bit_length() - 1)) == half
        bit_value = jnp.where(in_half, lax.shift_left(np.int32(1), slot & np.int32(HALF_BITS - 1)), 0)
        picked = pick(key, idx)
        halves = _dot_nt(bit_value.astype(F32).astype(BF16), picked.astype(BF16)).astype(I32)
        words_ref[...] = halves[:WORDS_PER_PAGE] | lax.shift_left(halves[WORDS_PER_PAGE:], np.int32(HALF_BITS))
        n_picked = jnp.sum(jnp.sum(picked, axis=1, keepdims=True), axis=0, keepdims=True)
        lane = lax.broadcasted_iota(I32, meta_ref.shape, 1)
        meta = jnp.where(lane == META_COUNT, n_picked, jnp.where(lane == META_NEW, pick(key_new, idx_new), 0.0))
        meta_ref[...] = meta.astype(I32)


def dsa_sample_select(iq3, coef3, ikb3, cache_kidx, page_table, layer):
    B, n_pages = page_table.shape
    G = min(PAGES_PER_STEP, n_pages)
    n_keys = n_pages * PAGE_SIZE + 1
    topk = min(TOPK_MAX, n_keys // 4)

    def page_spec(r):
        return pl.BlockSpec((None, None, PAGE_SIZE, IDX_DIM), lambda b, j, pt: (layer, pt[b, j * G + r], 0, 0))

    return pl.pallas_call(
        functools.partial(_dsa_select_body, topk=topk, nbits=n_keys.bit_length()),
        grid_spec=pltpu.PrefetchScalarGridSpec(
            num_scalar_prefetch=1,
            grid=(B, n_pages // G),
            in_specs=[pl.BlockSpec((None, IDX_HEADS, IDX_DIM), lambda b, j, pt: (b, 0, 0)),
                      pl.BlockSpec((None, IDX_HEADS, 1), lambda b, j, pt: (b, 0, 0)),
                      pl.BlockSpec((None, 1, IDX_DIM), lambda b, j, pt: (b, 0, 0))]
                     + [page_spec(r) for r in range(G)],
            out_specs=[pl.BlockSpec((None, WORDS_PER_PAGE, n_pages), lambda b, j, pt: (b, 0, 0)),
                       pl.BlockSpec((None, 1, 128), lambda b, j, pt: (b, 0, 0))],
            scratch_shapes=[pltpu.VMEM((n_pages, PAGE_SIZE), F32)],
        ),
        out_shape=[jax.ShapeDtypeStruct((B, WORDS_PER_PAGE, n_pages), I32),
                   jax.ShapeDtypeStruct((B, 1, 128), I32)],
        compiler_params=_cparams(("parallel", "arbitrary")),
        name="dsa_sample_select",
    )(page_table, iq3, coef3, ikb3, *([cache_kidx] * G))


def _row_copies(ck_ref, cv_ref, kbuf, vbuf, sem, layer, phys, slot, row):
    dst = pl.ds(pl.multiple_of(row * N_HEADS, N_HEADS), N_HEADS)
    return (pltpu.make_async_copy(ck_ref.at[layer, phys, slot], kbuf.at[dst, :], sem.at[0]),
            pltpu.make_async_copy(cv_ref.at[layer, phys, slot], vbuf.at[dst, :], sem.at[1]))


def _dsa_gather_body(pt_ref, ctz_ref, words_ref, meta_ref, q_ref, kn_ref, vn_ref, ck_ref, cv_ref, o_ref,
                     kbuf, vbuf, cnt_ref, sem, *, layer, cap):
    b = pl.program_id(0)
    n_pages = words_ref.shape[-1]
    want = jnp.minimum(meta_ref[0, META_COUNT], cap)
    kbuf[...] = jnp.zeros(kbuf.shape, F32)
    vbuf[...] = jnp.zeros(vbuf.shape, F32)
    cnt_ref[0] = 0

    def scan_page(carry):
        p, _ = carry
        words = [words_ref[w, p] for w in range(WORDS_PER_PAGE)]

        @pl.when(functools.reduce(lambda a, c: a | c, words) != 0)
        def _():
            phys = pt_ref[b, p]
            for w, word in enumerate(words):
                def next_bit(rest, w=w):
                    low = rest & (-rest)
                    bit = ctz_ref[lax.shift_right_logical(low * CTZ_MULTIPLIER, np.int32(27))]
                    n = cnt_ref[0]

                    @pl.when(n < want)
                    def _():
                        for cp in _row_copies(ck_ref, cv_ref, kbuf, vbuf, sem, layer, phys, w * WORD_BITS + bit, n):
                            cp.start()
                        cnt_ref[0] = n + 1

                    return rest ^ low

                lax.while_loop(lambda rest: rest != 0, next_bit, word)

        return p + 1, cnt_ref[0]

    _, cnt = lax.while_loop(lambda c: (c[0] < n_pages) & (c[1] < want), scan_page, (np.int32(0), np.int32(0)))

    def wait_loop(n, carry):
        for cp in _row_copies(ck_ref, cv_ref, kbuf, vbuf, sem, layer, 0, 0, n):
            cp.wait()
        return carry

    lax.fori_loop(0, cnt, wait_loop, 0)

    q8 = q_ref[...]
    valid = lax.broadcasted_iota(I32, (N_HEADS, cap), 1) < cnt
    seln = meta_ref[0, META_NEW] > 0
    sn = jnp.sum(q8.astype(F32) * kn_ref[...].astype(F32), axis=1, keepdims=True)
    vn = vn_ref[...].astype(F32)
    head = lax.broadcasted_iota(I32, (N_HEADS, HEAD_DIM), 0)
    out = jnp.zeros((N_HEADS, HEAD_DIM), F32)
    for h in range(N_HEADS):
        kh = kbuf[pl.ds(h, cap, stride=N_HEADS), :].astype(BF16)
        vh = vbuf[pl.ds(h, cap, stride=N_HEADS), :].astype(BF16)
        s = _dot_nt(q8, kh)
        m = jnp.maximum(jnp.max(jnp.where(valid, s, NEG_BIG), axis=1, keepdims=True), jnp.where(seln, sn, NEG_BIG))
        p = jnp.where(valid, jnp.exp(s - m), 0.0)
        pn = jnp.where(seln, jnp.exp(sn - m), 0.0)
        l = jnp.sum(p, axis=1, keepdims=True) + pn
        o = (_dot(p.astype(BF16), vh) + pn.astype(BF16).astype(F32) * vn) / l
        out = jnp.where(head == h, o, out)
    o_ref[...] = out.astype(BF16)


def dsa_sample_gather(words, meta, q8, kn8, vn8, cache_k, cache_v, page_table, layer):
    B, n_pages = page_table.shape
    cap = min(TOPK_MAX, (n_pages * PAGE_SIZE + 1) // 4)
    head_rows = pl.BlockSpec((None, N_HEADS, HEAD_DIM), lambda b, pt, ctz: (b, 0, 0))
    return pl.pallas_call(
        functools.partial(_dsa_gather_body, layer=layer, cap=cap),
        grid_spec=pltpu.PrefetchScalarGridSpec(
            num_scalar_prefetch=2,
            grid=(B,),
            in_specs=[pl.BlockSpec((None, WORDS_PER_PAGE, n_pages), lambda b, pt, ctz: (b, 0, 0),
                                   memory_space=pltpu.SMEM),
                      pl.BlockSpec((None, 1, 128), lambda b, pt, ctz: (b, 0, 0), memory_space=pltpu.SMEM),
                      head_rows, head_rows, head_rows,
                      pl.BlockSpec(memory_space=pl.ANY), pl.BlockSpec(memory_space=pl.ANY)],
            out_specs=head_rows,
            scratch_shapes=[pltpu.VMEM((cap * N_HEADS, HEAD_DIM), F32), pltpu.VMEM((cap * N_HEADS, HEAD_DIM), F32),
                            pltpu.SMEM((1,), I32), pltpu.SemaphoreType.DMA((2,))],
        ),
        out_shape=jax.ShapeDtypeStruct((B, N_HEADS, HEAD_DIM), BF16),
        compiler_params=_cparams(("arbitrary",)),
        name="dsa_sample_gather",
    )(page_table, jnp.asarray(CTZ_TABLE), words, meta, q8, kn8, vn8, cache_k, cache_v)


def _merge_body(x_ref, hm_ref, att_ref, ga_ref, gb_ref, wa_ref, wb_ref, wo_ref, o_ref):
    br_a = _dot(hm_ref[...], wa_ref[...])
    br_b = _dot(att_ref[...], wb_ref[...])
    t = jax.nn.sigmoid(ga_ref[...].astype(F32)) * br_a + jax.nn.sigmoid(gb_ref[...].astype(F32)) * br_b
    o_ref[...] = x_ref[...] + _dot(t.astype(BF16), wo_ref[...])


def merge(x, hm, att, zg, wa_all, wb_all, wo_all, layer, *, tm):
    M, D = x.shape
    wspec = lambda k: pl.BlockSpec((None, k, D), lambda i: (layer, 0, 0))
    return pl.pallas_call(
        _merge_body,
        grid=(M // tm,),
        in_specs=[pl.BlockSpec((tm, D), lambda i: (i, 0)),
                  pl.BlockSpec((tm, M_V), lambda i: (i, 0)),
                  pl.BlockSpec((tm, A_W), lambda i: (i, 0)),
                  pl.BlockSpec((tm, D), lambda i: (i, C_GA // D)),
                  pl.BlockSpec((tm, D), lambda i: (i, C_GB // D)),
                  wspec(M_V), wspec(A_W), wspec(D)],
        out_specs=pl.BlockSpec((tm, D), lambda i: (i, 0)),
        out_shape=jax.ShapeDtypeStruct((M, D), F32),
        compiler_params=_cparams(("parallel",)),
        name="merge",
    )(x, hm, att, zg, zg, wa_all, wb_all, wo_all)


S_MQ, S_MK, S_MV, S_MO, S_MI, S_MF = 0, 512, 1024, 2048, 3072, 3076
S_AQ, S_AK, S_AV, S_IQ, S_IK, S_IW, S_GA, S_GB, S_END = 3080, 4104, 5128, 6152, 7176, 7240, 7256, 9304, 11352
G_SOURCES = ((S_GA, D_MODEL), (S_GB, D_MODEL), (S_MQ, M_QK), (S_MK, M_QK), (S_MV, M_V), (S_MO, M_V))
A_SOURCES = ((S_AQ, A_W), (S_AK, A_W), (S_AV, A_W), (S_IQ, IDX_QW))
LANE_SHIFTS = (0, S_AQ % 128, S_GA % 128)


def _relayout_body(a_tab, b_tab, s_tab, wa_ref, wb_ref, tail_ref, o_ref):
    j = pl.program_id(1)
    shift = s_tab[j]
    W = wa_ref.shape[-1]
    x = jnp.concatenate([wa_ref[...], wb_ref[...]], axis=1)
    for s in LANE_SHIFTS:
        @pl.when(shift == s)
        def _():
            y = x if s == 0 else pltpu.roll(x, W + 128 - s, 1)
            o_ref[...] = y[:, :W].astype(BF16)

    @pl.when(shift < 0)
    def _():
        o_ref[...] = tail_ref[...]


def _relayout(w_in, sources, tail):
    depth, D, _ = w_in.shape
    W = RELAYOUT_W
    starts = [c0 + k * W for c0, width in sources for k in range(width // W)]
    a_tab = [c // W for c in starts] + [0] * (tail is not None)
    b_tab = [(c // W) * (W // 128) + W // 128 for c in starts] + [0] * (tail is not None)
    s_tab = [c % 128 for c in starts] + [-1] * (tail is not None)
    assert all(c % W - c % 128 == 0 and c % 128 in LANE_SHIFTS for c in starts)
    if tail is None:
        tail = jnp.zeros((depth, D, W), BF16)
    n = len(s_tab)
    tabs = [jnp.asarray(np.array(t, np.int32)) for t in (a_tab, b_tab, s_tab)]
    return pl.pallas_call(
        _relayout_body,
        grid_spec=pltpu.PrefetchScalarGridSpec(
            num_scalar_prefetch=3,
            grid=(depth, n),
            in_specs=[pl.BlockSpec((None, D, W), lambda l, j, a, b, s: (l, 0, a[j])),
                      pl.BlockSpec((None, D, 128), lambda l, j, a, b, s: (l, 0, b[j])),
                      pl.BlockSpec((None, D, W), lambda l, j, a, b, s: (l, 0, 0))],
            out_specs=pl.BlockSpec((None, D, W), lambda l, j, a, b, s: (l, 0, j)),
        ),
        out_shape=jax.ShapeDtypeStruct((depth, D, n * W), BF16),
        compiler_params=_cparams(("parallel", "arbitrary")),
        name="relayout_w_in",
    )(*tabs, w_in, w_in, tail)


def relayout_w_in(w_in, b_in):
    def small(a):
        pad = jnp.zeros(a.shape[:-1] + (RELAYOUT_W - (L_MF + M_HEADS),), a.dtype)
        return jnp.concatenate([a[..., S_IK:S_GA], a[..., S_MI:S_AQ], pad], axis=-1)

    take = lambda a, sources: [a[..., c0:c0 + width] for c0, width in sources]
    bg = jnp.concatenate(take(b_in, G_SOURCES), axis=-1)
    ba = jnp.concatenate(take(b_in, A_SOURCES) + [small(b_in)], axis=-1)
    wg = _relayout(w_in, G_SOURCES, None)
    wa = _relayout(w_in, A_SOURCES, small(w_in).astype(BF16))
    return wg, wa, bg[:, None, :], ba[:, None, :]


FFN_ROWS = 1024
FFN_COLS = 256
INPROJ_G_COLS = 1024
INPROJ_A_ROWS = 1024
INPROJ_A_COLS = NZ_A // 3
POST_ROWS = 512
MLSTM_CHUNK = 256
SAMPLE_PAD = 16


def _mixer_block(x, layer, P, tabs, kv_rows, *, depth, tm, ta, tq, g_dtype, tab_blocks, transposed, mixers):
    zg = inproj(x, P["g_mix"], P["wg"], P["bg"], layer, tm=tm, tn=INPROJ_G_COLS, out_dtype=g_dtype)
    za = inproj(x, P["g_mix"], P["wa"], P["ba"], layer, tm=ta, tn=INPROJ_A_COLS, out_dtype=F32)
    post = qkpost(za, P["g_q"], P["g_k"], P["g_ik"], tabs, layer, kv_rows, depth=depth, tm=tq,
                  tab_blocks=tab_blocks, transposed=transposed)
    kv_rows, miscr = (post[1], post[3]), post[6]
    hm, att, state = mixers(zg, za, post)
    x = merge(x, hm, att, zg, P["w_a"], P["w_b"], P["w_o"], layer, tm=tq)
    return x, kv_rows, (miscr[:, :IDX_DIM],) + state


def kernel(x_prompt, x_sample, cache_k, cache_v, cache_kidx, state_C, state_n, state_m, page_table, norm_ffn1, w_ffn1_gu, w_ffn1_down, norm_mix, w_in, b_in, q_norm, k_norm, idx_k_norm, mlstm_norm, w_branch_a, w_branch_b, w_out, norm_ffn2, w_ffn2_gu, w_ffn2_down):
    B, S, D = x_prompt.shape
    BS = x_sample.shape[0]
    depth = w_in.shape[0]
    past = page_table.shape[1] * PAGE_SIZE

    wg, wa, bg, ba = relayout_w_in(w_in, b_in)
    pad128 = lambda g: jnp.pad(g, ((0, 0), (0, 128 - g.shape[-1])))[:, None, :]
    P = dict(g_f1=norm_ffn1[:, None, :], w_f1_gu=w_ffn1_gu, w_f1_dn=w_ffn1_down,
             g_mix=norm_mix[:, None, :], wg=wg, wa=wa, bg=bg, ba=ba,
             g_q=q_norm[:, None, :], g_k=k_norm[:, None, :], g_ik=pad128(idx_k_norm),
             g_m=mlstm_norm.reshape(depth, 1, M_V),
             w_a=w_branch_a.astype(BF16), w_b=w_branch_b.astype(BF16), w_o=w_out.astype(BF16),
             g_f2=norm_ffn2[:, None, :], w_f2_gu=w_ffn2_gu, w_f2_dn=w_ffn2_down)
    tabs_p = rope_tables(jnp.arange(S))
    tabs_s = rope_tables(jnp.full((BS,), past, jnp.int32))

    xp = x_prompt.reshape(B * S, D)
    xs = x_sample.reshape(BS, D)
    pad_rows = lambda a: jnp.pad(a, ((0, SAMPLE_PAD - BS), (0, 0)))
    tm = min(FFN_ROWS, B * S)
    ta = min(INPROJ_A_ROWS, B * S)
    tq = min(POST_ROWS, B * S)
    rows_p, rows_s = [], []
    kv_p = tuple(jnp.zeros((depth, B * S, N_HEADS, HEAD_DIM), F32) for _ in range(2))
    kv_s = tuple(jnp.zeros((depth, BS, N_HEADS, HEAD_DIM), F32) for _ in range(2))
    for layer in range(depth):
        def mix_prompt(zg, za, post):
            qt, _, kb, _, vt, iqt, miscr, misct = post
            hm, C, n, m = mlstm_prompt(zg, za, P["g_m"], layer, B=B, S=S, T=MLSTM_CHUNK)
            att = dsa_prompt(qt, iqt, misct, miscr, kb, vt, B=B, S=S)
            return hm, att, (C, n, m)

        def mix_sample(zg, za, post):
            qb, _, kb, _, vb, iqb, miscr = post
            hm, C, n, m = mlstm_step(zg, za, P["g_m"], state_C, state_n, state_m, layer)
            iq3 = iqb.reshape(BS, IDX_HEADS, IDX_DIM)
            coef3 = miscr[:, L_IW:L_IW + IDX_HEADS].reshape(BS, IDX_HEADS, 1)
            ikb3 = miscr[:, :IDX_DIM].astype(BF16).reshape(BS, 1, IDX_DIM)
            words, meta = dsa_sample_select(iq3, coef3, ikb3, cache_kidx, page_table, layer)
            heads = lambda a: a.reshape(BS, N_HEADS, HEAD_DIM)
            att = dsa_sample_gather(words, meta, heads(qb), heads(kb), heads(vb), cache_k, cache_v, page_table, layer)
            return hm, att.reshape(BS, A_W), (C, n, m)

        xp, xs16 = ffn(xp, pad_rows(xs), P["g_f1"], P["w_f1_gu"], P["w_f1_dn"], layer, tm=tm, tf=FFN_COLS)
        xp, kv_p, rp = _mixer_block(xp, layer, P, tabs_p, kv_p, depth=depth, tm=tm, ta=ta, tq=tq, g_dtype=BF16,
                                    tab_blocks=S // tq, transposed=True, mixers=mix_prompt)
        xs, kv_s, rs = _mixer_block(xs16[:BS], layer, P, tabs_s, kv_s, depth=depth, tm=BS, ta=BS, tq=BS, g_dtype=F32,
                                    tab_blocks=1, transposed=False, mixers=mix_sample)
        xp, xs16 = ffn(xp, pad_rows(xs), P["g_f2"], P["w_f2_gu"], P["w_f2_dn"], layer, tm=tm, tf=FFN_COLS)
        xs = xs16[:BS]
        rows_p.append(rp)
        rows_s.append(rs)

    def stack(rows, idx, shape):
        return jnp.stack([r[idx] for r in rows]).reshape((depth,) + shape)

    return (xp.reshape(B, S, D), xs.reshape(BS, 1, D),
            kv_p[0].reshape(depth, B, S, N_HEADS, HEAD_DIM), kv_p[1].reshape(depth, B, S, N_HEADS, HEAD_DIM),
            stack(rows_p, 0, (B, S, IDX_DIM)),
            stack(rows_p, 1, (B, M_HEADS, M_QK_DIM, M_V_DIM)), stack(rows_p, 2, (B, M_HEADS, M_QK_DIM)),
            stack(rows_p, 3, (B, M_HEADS)),
            kv_s[0].reshape(depth, BS, 1, N_HEADS, HEAD_DIM), kv_s[1].reshape(depth, BS, 1, N_HEADS, HEAD_DIM),
            stack(rows_s, 0, (BS, 1, IDX_DIM)),
            stack(rows_s, 1, (BS, M_HEADS, M_QK_DIM, M_V_DIM)), stack(rows_s, 2, (BS, M_HEADS, M_QK_DIM)),
            stack(rows_s, 3, (BS, M_HEADS)))
```

```python
import functools

import jax
import jax.numpy as jnp
import numpy as np
from jax import lax
from jax.experimental import pallas as pl
from jax.experimental.pallas import tpu as pltpu

F32 = jnp.float32
BF16 = jnp.bfloat16
I32 = jnp.int32

EPS = 1e-6
ROPE_THETA = 500000.0
PAGE_SIZE = 128
TOPK_MAX = 256

M_HEADS = 4
M_QK_DIM = 128
M_V_DIM = 256
N_HEADS = 8
HEAD_DIM = 128
IDX_HEADS = 16
IDX_DIM = 64
D_MODEL = 2048

M_QK = M_HEADS * M_QK_DIM
M_V = M_HEADS * M_V_DIM
A_W = N_HEADS * HEAD_DIM
IDX_QW = IDX_HEADS * IDX_DIM

C_GA = 0
C_GB = C_GA + D_MODEL
C_MQ = C_GB + D_MODEL
C_MK = C_MQ + M_QK
C_MV = C_MK + M_QK
C_MO = C_MV + M_V
NZ_G = C_MO + M_V
C_AQ = 0
C_AK = C_AQ + A_W
C_AV = C_AK + A_W
C_IQ = C_AV + A_W
C_MISC = C_IQ + IDX_QW
L_IK = 0
L_IW = IDX_DIM
L_MI = L_IW + IDX_HEADS
L_MF = L_MI + M_HEADS
MISC_W = 128
RELAYOUT_W = 512
NZ_A = C_MISC + RELAYOUT_W

VMEM_LIMIT = 58 * 1024 * 1024

NEG_BIG = -1e30
MAX_INIT = -1e20
KEY_TILE = 256


def _cparams(sem, vmem=VMEM_LIMIT):
    return pltpu.CompilerParams(dimension_semantics=sem, vmem_limit_bytes=vmem)


def _rms(x, g):
    return x * lax.rsqrt(jnp.mean(x * x, axis=-1, keepdims=True) + EPS) * g


def _dot(a, b):
    return jnp.dot(a, b, preferred_element_type=F32)


def _dot_nt(a, b):
    return lax.dot_general(a, b, (((1,), (1,)), ((), ())), preferred_element_type=F32)


def _log_sigmoid(x):
    return jnp.minimum(x, 0.0) - jnp.log1p(jnp.exp(-jnp.abs(x)))


def _ffn_body(x_ref, xs_ref, g_ref, wg_ref, wu_ref, wd_ref, o_ref, os_ref, h_ref):
    i, j = pl.program_id(0), pl.program_id(1)
    tm = x_ref.shape[0]

    @pl.when(j == 0)
    def _():
        x = x_ref[...]
        h_ref[:tm, :] = _rms(x, g_ref[...]).astype(BF16)
        o_ref[...] = x

    @pl.when((i == 0) & (j == 0))
    def _():
        xs = xs_ref[...]
        h_ref[tm:, :] = _rms(xs, g_ref[...]).astype(BF16)
        os_ref[...] = xs

    h = h_ref[...]
    a = _dot(h, wg_ref[...].astype(BF16))
    u = _dot(h, wu_ref[...].astype(BF16))
    p = (0.5 * a) * jax.nn.sigmoid(a) * u
    r = _dot(p.astype(BF16), wd_ref[...].astype(BF16))
    o_ref[...] += r[:tm]

    @pl.when(i == 0)
    def _():
        os_ref[...] += r[tm:]


def ffn(x, xs, g_all, wgu_all, wdn_all, layer, *, tm, tf):
    M, D = x.shape
    MS = xs.shape[0]
    F = wdn_all.shape[1]
    nf = F // tf
    return pl.pallas_call(
        _ffn_body,
        grid=(M // tm, nf),
        in_specs=[
            pl.BlockSpec((tm, D), lambda i, j: (i, 0)),
            pl.BlockSpec((MS, D), lambda i, j: (0, 0)),
            pl.BlockSpec((None, 1, D), lambda i, j: (layer, 0, 0)),
            pl.BlockSpec((None, D, tf), lambda i, j: (layer, 0, j)),
            pl.BlockSpec((None, D, tf), lambda i, j: (layer, 0, j + nf)),
            pl.BlockSpec((None, tf, D), lambda i, j: (layer, j, 0)),
        ],
        out_specs=[pl.BlockSpec((tm, D), lambda i, j: (i, 0)),
                   pl.BlockSpec((MS, D), lambda i, j: (0, 0))],
        out_shape=[jax.ShapeDtypeStruct((M, D), F32), jax.ShapeDtypeStruct((MS, D), F32)],
        scratch_shapes=[pltpu.VMEM((tm + MS, D), BF16)],
        compiler_params=_cparams(("arbitrary", "arbitrary")),
        name="ffn",
    )(x, xs, g_all, wgu_all, wgu_all, wdn_all)


def _inproj_body(x_ref, g_ref, w_ref, b_ref, z_ref, h_ref):
    @pl.when(pl.program_id(1) == 0)
    def _():
        h_ref[...] = _rms(x_ref[...], g_ref[...]).astype(BF16)

    z_ref[...] = (_dot(h_ref[...], w_ref[...]) + b_ref[...]).astype(z_ref.dtype)


def inproj(x, g_all, wr_all, br_all, layer, *, tm, tn, out_dtype):
    M, D = x.shape
    NZ = wr_all.shape[-1]
    return pl.pallas_call(
        _inproj_body,
        grid=(M // tm, NZ // tn),
        in_specs=[
            pl.BlockSpec((tm, D), lambda i, j: (i, 0)),
            pl.BlockSpec((None, 1, D), lambda i, j: (layer, 0, 0)),
            pl.BlockSpec((None, D, tn), lambda i, j: (layer, 0, j)),
            pl.BlockSpec((None, 1, tn), lambda i, j: (layer, 0, j)),
        ],
        out_specs=pl.BlockSpec((tm, tn), lambda i, j: (i, j)),
        out_shape=jax.ShapeDtypeStruct((M, NZ), out_dtype),
        scratch_shapes=[pltpu.VMEM((tm, D), BF16)],
        compiler_params=_cparams(("parallel", "arbitrary")),
        name="inproj",
    )(x, g_all, wr_all, br_all)


def _rope(x, c, sa, sb, shift):
    n = x.shape[-1]
    return x * c + pltpu.roll(x, n - shift, 1) * sa + pltpu.roll(x, shift, 1) * sb


def _qkpost_body(*refs, transposed, n_carried):
    (aq_ref, ak_ref, av_ref, iq_ref, misc_ref, gq_ref, gk_ref, gik_ref,
     c128_ref, sa128_ref, sb128_ref, c64_ref, sa64_ref, sb64_ref) = refs[:14]
    q_ref, kf_ref, kb_ref, vf_ref, v_ref, iqo_ref, miscr_ref, *misct_ref = refs[14 + n_carried:]
    c128, sa128, sb128 = c128_ref[...], sa128_ref[...], sb128_ref[...]
    c64, sa64, sb64 = c64_ref[...], sa64_ref[...], sb64_ref[...]
    gq, gk = gq_ref[...], gk_ref[...]
    half128 = HEAD_DIM // 8
    half64 = IDX_DIM // 8
    tm = aq_ref.shape[0]
    for h in range(N_HEADS):
        sl = slice(h * HEAD_DIM, (h + 1) * HEAD_DIM)
        q = _rope(_rms(aq_ref[:, sl], gq), c128, sa128, sb128, half128) * (HEAD_DIM ** -0.5)
        k = _rope(_rms(ak_ref[:, sl], gk), c128, sa128, sb128, half128)
        kf_ref[:, h, :] = k
        kb_ref[:, sl] = k.astype(BF16)
        vf_ref[:, h, :] = av_ref[:, sl]
        if transposed:
            q_ref[sl, :] = q.T.astype(BF16)
            for t in range(tm // KEY_TILE):
                v_ref[t, sl, :] = av_ref[t * KEY_TILE:(t + 1) * KEY_TILE, sl].T.astype(BF16)
        else:
            q_ref[:, sl] = q.astype(BF16)
            v_ref[:, sl] = av_ref[:, sl].astype(BF16)
    for h in range(IDX_QW // 128):
        sl = slice(h * 128, (h + 1) * 128)
        iq = _rope(iq_ref[:, sl], c64, sa64, sb64, half64)
        if transposed:
            iqo_ref[sl, :] = iq.T.astype(BF16)
        else:
            iqo_ref[:, sl] = iq.astype(BF16)
    misc = misc_ref[...]
    lane = lax.broadcasted_iota(I32, misc.shape, 1)
    is_ik = lane < IDX_DIM
    ikv = jnp.where(is_ik, misc, 0.0)
    ms = jnp.sum(ikv * ikv, axis=-1, keepdims=True) * (1.0 / IDX_DIM)
    ikn = ikv * lax.rsqrt(ms + EPS) * gik_ref[...]
    ikr = _rope(ikn, c64, sa64, sb64, half64)
    is_iw = (lane >= L_IW) & (lane < L_IW + IDX_HEADS)
    iw_scale = (IDX_HEADS ** -0.5) * (IDX_DIM ** -0.5)
    miscr = jnp.where(is_ik, ikr, jnp.where(is_iw, misc * iw_scale, misc))
    miscr_ref[...] = miscr
    if transposed:
        misct_ref[0][...] = miscr.T


def qkpost(z, gq_all, gk_all, gik_all, tabs, layer, carried, *, depth, tm, tab_blocks, transposed):
    M = z.shape[0]
    zspec = lambda w, c: pl.BlockSpec((tm, w), lambda i: (i, c // w))
    tspec = pl.BlockSpec((tm, 128), lambda i: (i % tab_blocks, 0))
    gspec = pl.BlockSpec((None, 1, 128), lambda i: (layer, 0, 0))
    row = lambda w: pl.BlockSpec((tm, w), lambda i: (i, 0))
    col = lambda w: pl.BlockSpec((w, tm), lambda i: (0, i))
    rows_spec = pl.BlockSpec((None, tm, N_HEADS, HEAD_DIM), lambda i: (layer, i, 0, 0))
    sds = jax.ShapeDtypeStruct
    rows_shape = sds((depth, M, N_HEADS, HEAD_DIM), F32)
    if transposed:
        nt = tm // KEY_TILE
        out_specs = [col(A_W), rows_spec, row(A_W), rows_spec, pl.BlockSpec((nt, A_W, KEY_TILE), lambda i: (i, 0, 0)),
                     col(IDX_QW), row(MISC_W), col(MISC_W)]
        out_shape = [sds((A_W, M), BF16), rows_shape, sds((M, A_W), BF16), rows_shape,
                     sds((M // KEY_TILE, A_W, KEY_TILE), BF16), sds((IDX_QW, M), BF16), sds((M, MISC_W), F32),
                     sds((MISC_W, M), F32)]
    else:
        out_specs = [row(A_W), rows_spec, row(A_W), rows_spec, row(A_W), row(IDX_QW), row(MISC_W)]
        out_shape = [sds((M, A_W), BF16), rows_shape, sds((M, A_W), BF16), rows_shape,
                     sds((M, A_W), BF16), sds((M, IDX_QW), BF16), sds((M, MISC_W), F32)]
    n_in = 14
    return pl.pallas_call(
        functools.partial(_qkpost_body, transposed=transposed, n_carried=len(carried)),
        grid=(M // tm,),
        in_specs=[zspec(A_W, C_AQ), zspec(A_W, C_AK), zspec(A_W, C_AV), zspec(IDX_QW, C_IQ),
                  zspec(MISC_W, C_MISC), gspec, gspec, gspec] + [tspec] * 6
                 + [pl.BlockSpec(memory_space=pl.ANY)] * len(carried),
        out_specs=out_specs,
        out_shape=out_shape,
        input_output_aliases={n_in + c: o for c, o in zip(range(len(carried)), (1, 3))},
        compiler_params=_cparams(("parallel",)),
        name="qkpost",
    )(z, z, z, z, z, gq_all, gk_all, gik_all, *tabs, *carried)


def rope_tables(pos):
    pos = pos.astype(F32)[:, None]

    def tables(width):
        rd = width // 4
        half = rd // 2
        inv = ROPE_THETA ** (-jnp.arange(half, dtype=F32) * (2.0 / rd))
        lane = np.arange(128) % width
        ang = pos * inv[None, :][:, lane % half]
        cos, sin = jnp.cos(ang), jnp.sin(ang)
        lo = jnp.asarray(lane < half)[None, :]
        hi = jnp.asarray((lane >= half) & (lane < rd))[None, :]
        c = jnp.where(lo | hi, cos, 1.0)
        sa = jnp.where(lo, -sin, 0.0)
        sb = jnp.where(hi, sin, 0.0)
        return c, sa, sb

    return tables(HEAD_DIM) + tables(IDX_DIM)


def _split3(x):
    hi = x.astype(BF16)
    r1 = x - hi.astype(F32)
    mid = r1.astype(BF16)
    lo = (r1 - mid.astype(F32)).astype(BF16)
    return hi, mid, lo


def _mlstm_body(zq_ref, zk_ref, zv_ref, zo_ref, misc_ref, gm_ref, s0_ref, hm_ref, sout_ref, s_ref, m_ref):
    c = pl.program_id(1)
    T = zq_ref.shape[0]
    EW = s_ref.shape[-1]

    @pl.when(c == 0)
    def _():
        s_ref[...] = s0_ref[...]
        for h in range(M_HEADS):
            m_ref[h:h + 1, :] = jnp.broadcast_to(s0_ref[h, 0:1, M_V_DIM + 1:M_V_DIM + 2], (1, 128))

    misc = misc_ref[...]
    misc_t = misc.T
    gates_t = misc_t[L_MI:L_MI + 8, :]
    lf_rows = _log_sigmoid(gates_t)
    lf_cols = _log_sigmoid(misc)
    ti = lax.broadcasted_iota(I32, (T, T), 0)
    si = lax.broadcasted_iota(I32, (T, T), 1)
    causal = si <= ti
    ltri = jnp.where(causal, 1.0, 0.0).astype(BF16)
    utri = jnp.where(ti <= si, 1.0, 0.0).astype(BF16)
    b_cols = sum(_dot(ltri, p) for p in _split3(lf_cols))
    b_rows = sum(_dot(p, utri) for p in _split3(lf_rows))
    ones_col = jnp.where(lax.broadcasted_iota(I32, (T, EW - M_V_DIM), 1) == 0, 1.0, 0.0).astype(BF16)

    for h in range(M_HEADS):
        q = (zq_ref[:, h * M_QK_DIM:(h + 1) * M_QK_DIM].astype(F32) * (M_QK_DIM ** -0.5)).astype(BF16)
        k = zk_ref[:, h * M_QK_DIM:(h + 1) * M_QK_DIM].astype(F32)
        ve = jnp.concatenate([zv_ref[:, h * M_V_DIM:(h + 1) * M_V_DIM].astype(BF16), ones_col], axis=1)
        bcol = b_cols[:, L_MF + h:L_MF + h + 1]
        brow = b_rows[M_HEADS + h:M_HEADS + h + 1, :]
        r = gates_t[h:h + 1, :] - brow
        m0 = m_ref[h:h + 1, 0:1]
        d = jnp.where(causal, bcol + r, -jnp.inf)
        inter = bcol + m0
        m = jnp.maximum(inter, jnp.max(d, axis=1, keepdims=True))
        w = (_dot_nt(q, k.astype(BF16)) * jnp.exp(d - m)).astype(BF16)
        c_inter = jnp.exp(inter - m)
        s0 = s_ref[h]
        nd = c_inter * _dot(q, s0.astype(BF16)) + _dot(w, ve)
        hh = nd[:, :M_V_DIM] / jnp.maximum(jnp.abs(nd[:, M_V_DIM:M_V_DIM + 1]), jnp.exp(-m))
        sl = slice(h * M_V_DIM, (h + 1) * M_V_DIM)
        hm_ref[:, sl] = (_rms(hh, gm_ref[:, sl]) * jax.nn.sigmoid(zo_ref[:, sl].astype(F32))).astype(BF16)

        b_end = brow[:, T - 1:T]
        g = b_end + r
        m_new = jnp.maximum(b_end + m0, jnp.max(g, axis=1, keepdims=True))
        a = jnp.exp(g - m_new)
        c_old = jnp.exp(b_end + m0 - m_new)
        s_ref[h] = c_old * s0 + _dot((k.T * a).astype(BF16), ve)
        m_ref[h:h + 1, :] = jnp.broadcast_to(m_new, (1, 128))

    @pl.when(c == pl.num_programs(1) - 1)
    def _():
        ri = lax.broadcasted_iota(I32, s_ref.shape[1:], 0)
        ci = lax.broadcasted_iota(I32, s_ref.shape[1:], 1)
        for h in range(M_HEADS):
            sout_ref[h] = jnp.where((ri == 0) & (ci == M_V_DIM + 1), m_ref[h:h + 1, 0:1], s_ref[h])


def mlstm_prompt(zg, za, gm_all, layer, *, B, S, T):
    EW = M_V_DIM + 128
    nc = S // T
    s0 = jnp.zeros((M_HEADS, M_QK_DIM, EW), F32)
    zspec = lambda w, c0: pl.BlockSpec((T, w), lambda b, c: (b * nc + c, c0 // w))
    hm, sout = pl.pallas_call(
        _mlstm_body,
        grid=(B, nc),
        in_specs=[zspec(M_QK, C_MQ), zspec(M_QK, C_MK), zspec(M_V, C_MV), zspec(M_V, C_MO), zspec(MISC_W, C_MISC),
                  pl.BlockSpec((None, 1, M_V), lambda b, c: (layer, 0, 0)),
                  pl.BlockSpec((M_HEADS, M_QK_DIM, EW), lambda b, c: (0, 0, 0))],
        out_specs=[pl.BlockSpec((T, M_V), lambda b, c: (b * nc + c, 0)),
                   pl.BlockSpec((None, M_HEADS, M_QK_DIM, EW), lambda b, c: (b, 0, 0, 0))],
        out_shape=[jax.ShapeDtypeStruct((B * S, M_V), BF16),
                   jax.ShapeDtypeStruct((B, M_HEADS, M_QK_DIM, EW), F32)],
        scratch_shapes=[pltpu.VMEM((M_HEADS, M_QK_DIM, EW), F32), pltpu.VMEM((8, 128), F32)],
        compiler_params=_cparams(("parallel", "arbitrary")),
        name="mlstm_prompt",
    )(zg, zg, zg, zg, za, gm_all, s0)
    return hm, sout[..., :M_V_DIM], sout[..., M_V_DIM], sout[:, :, 0, M_V_DIM + 1]


def _mlstm_step_body(zq_ref, zk_ref, zv_ref, zo_ref, misc_ref, gm_ref, c_ref, n_ref, m_ref,
                     hm_ref, cout_ref, nout_ref, mout_ref):
    misc = misc_ref[...]
    lf_all = _log_sigmoid(misc)
    eye = lax.broadcasted_iota(I32, (M_QK_DIM, M_QK_DIM), 0) == lax.broadcasted_iota(I32, (M_QK_DIM, M_QK_DIM), 1)
    for h in range(M_HEADS):
        q = zq_ref[:, h * M_QK_DIM:(h + 1) * M_QK_DIM].astype(F32) * (M_QK_DIM ** -0.5)
        k = zk_ref[:, h * M_QK_DIM:(h + 1) * M_QK_DIM].astype(F32)
        sl = slice(h * M_V_DIM, (h + 1) * M_V_DIM)
        v = zv_ref[:, sl].astype(F32)
        li = misc[:, L_MI + h:L_MI + h + 1]
        lf = lf_all[:, L_MF + h:L_MF + h + 1]
        s0 = c_ref[h]
        n0 = n_ref[h:h + 1, :]
        m0 = m_ref[:, h:h + 1]
        m = jnp.maximum(lf + m0, li)
        w = jnp.sum(q * k, axis=1, keepdims=True) * jnp.exp(li - m)
        c_inter = jnp.exp(lf + m0 - m)
        q8 = jnp.broadcast_to(q, (8, M_QK_DIM)).astype(BF16)
        qs = _dot(q8, s0.astype(BF16))[0:1, :]
        num = c_inter * qs + w * v
        den = c_inter * jnp.sum(q * n0, axis=1, keepdims=True) + w
        hh = num / jnp.maximum(jnp.abs(den), jnp.exp(-m))
        hm_ref[:, sl] = (_rms(hh, gm_ref[:, sl]) * jax.nn.sigmoid(zo_ref[:, sl].astype(F32))).astype(BF16)
        a = jnp.exp(li - m)
        kcol = jnp.sum(jnp.where(eye, jnp.broadcast_to(k, (M_QK_DIM, M_QK_DIM)), 0.0), axis=1, keepdims=True)
        cout_ref[h] = c_inter * s0 + (a * kcol) * v
        nout_ref[h:h + 1, :] = c_inter * n0 + a * k
        mout_ref[:, h:h + 1] = m


def mlstm_step(zg, za, gm_all, state_c, state_n, state_m, layer):
    B = zg.shape[0]
    zg3 = zg.reshape(B, 1, NZ_G)
    za3 = za.reshape(B, 1, NZ_A)
    m4 = state_m.reshape(state_m.shape[0], B, 1, M_HEADS)
    zspec = lambda w, c0: pl.BlockSpec((None, 1, w), lambda b: (b, 0, c0 // w))
    hm, cout, nout, mout = pl.pallas_call(
        _mlstm_step_body,
        grid=(B,),
        in_specs=[zspec(M_QK, C_MQ), zspec(M_QK, C_MK), zspec(M_V, C_MV), zspec(M_V, C_MO), zspec(MISC_W, C_MISC),
                  pl.BlockSpec((None, 1, M_V), lambda b: (layer, 0, 0)),
                  pl.BlockSpec((None, None, M_HEADS, M_QK_DIM, M_V_DIM), lambda b: (layer, b, 0, 0, 0)),
                  pl.BlockSpec((None, None, M_HEADS, M_QK_DIM), lambda b: (layer, b, 0, 0)),
                  pl.BlockSpec((None, None, 1, M_HEADS), lambda b: (layer, b, 0, 0))],
        out_specs=[pl.BlockSpec((None, 1, M_V), lambda b: (b, 0, 0)),
                   pl.BlockSpec((None, M_HEADS, M_QK_DIM, M_V_DIM), lambda b: (b, 0, 0, 0)),
                   pl.BlockSpec((None, M_HEADS, M_QK_DIM), lambda b: (b, 0, 0)),
                   pl.BlockSpec((None, 1, M_HEADS), lambda b: (b, 0, 0))],
        out_shape=[jax.ShapeDtypeStruct((B, 1, M_V), BF16),
                   jax.ShapeDtypeStruct((B, M_HEADS, M_QK_DIM, M_V_DIM), F32),
                   jax.ShapeDtypeStruct((B, M_HEADS, M_QK_DIM), F32),
                   jax.ShapeDtypeStruct((B, 1, M_HEADS), F32)],
        compiler_params=_cparams(("parallel",)),
        name="mlstm_step",
    )(zg3, zg3, zg3, zg3, za3, gm_all, state_c, state_n, m4)
    return hm.reshape(B, M_V), cout, nout, mout.reshape(B, M_HEADS)


_INT_MIN = np.int32(-2 ** 31)
_KEY_NEG_INF = np.int32(np.array(0xFF800000, np.uint32).view(np.int32) ^ np.int32(0x7FFFFFFF))


def _sort_key(x):
    bits = pltpu.bitcast(x, I32)
    return jnp.where(bits < 0, bits ^ np.int32(0x7FFFFFFF), bits)


def _topk_select(count, topk, nbits, shape):
    def bit_step(b, theta_u):
        cand_u = theta_u | lax.shift_left(np.int32(1), np.int32(31) - b)
        cand_s = cand_u ^ _INT_MIN
        cnt = count(lambda key, idx: jnp.where(key >= cand_s, 1.0, 0.0))
        return jnp.where(cnt >= float(topk), cand_u, theta_u)

    theta = lax.fori_loop(0, 32, bit_step, jnp.zeros(shape, I32)) ^ _INT_MIN
    room = float(topk) - count(lambda key, idx: jnp.where(key > theta, 1.0, 0.0))

    def tie_step(b, jlim):
        cand = jlim | lax.shift_left(np.int32(1), np.int32(nbits - 1) - b)
        f = count(lambda key, idx: jnp.where(key == theta, jnp.where(idx < cand, 1.0, 0.0), 0.0))
        return jnp.where(f <= room, cand, jlim)

    return theta, room, tie_step


def _dsa_prompt_body(qt_ref, iqt_ref, misct_ref, k_ref, vt_ref, misck_ref, o_ref,
                     key_ref, jlim_ref, m_ref, l_ref, alpha_ref, acc_ref, bias_ref, s_ref, p_ref, *, topk, nbits):
    i = pl.program_id(1)
    KT, QB = key_ref.shape[1:]
    nt = i + 1
    kloc = lax.broadcasted_iota(I32, (KT, QB), 0)
    qpos = i * QB + lax.broadcasted_iota(I32, (KT, QB), 1)
    coef = misct_ref[L_IW:L_IW + IDX_HEADS, :]

    def score_tile(j, carry):
        off = pl.multiple_of(j * KT, KT)
        ik = misck_ref[pl.ds(off, KT), :][:, :IDX_DIM].astype(BF16)
        sc = jnp.zeros((KT, QB), F32)
        for h in range(IDX_HEADS):
            d = _dot(ik, iqt_ref[h * IDX_DIM:(h + 1) * IDX_DIM, :])
            sc = sc + jnp.maximum(d, 0.0) * coef[h:h + 1, :]
        sc = jnp.where(j * KT + kloc <= qpos, sc, -jnp.inf)
        key_ref[j] = _sort_key(sc)
        return carry

    lax.fori_loop(0, nt, score_tile, 0)

    def count(f):
        def body(j, cnt8):
            return cnt8 + jnp.sum(f(key_ref[j], j * KT + kloc).reshape(KT // 8, 8, QB), axis=0)

        return jnp.sum(lax.fori_loop(0, nt, body, jnp.zeros((8, QB), F32)), axis=0, keepdims=True)

    theta, room, tie_step = _topk_select(count, topk, nbits, (1, QB))
    n_tie = count(lambda key, idx: jnp.where(key == theta, 1.0, 0.0))
    jlim_ref[...] = jnp.full((1, QB), 2 ** nbits, I32)
    crowded = jnp.max(jnp.where((n_tie > room) & (theta > _KEY_NEG_INF), 1.0, 0.0))

    @pl.when(crowded > 0.0)
    def _():
        jlim_ref[...] = lax.fori_loop(0, nbits, tie_step, jnp.zeros((1, QB), I32))

    jlim = jlim_ref[...]

    def finalize(j, carry):
        key = key_ref[j]
        sel = jnp.where(key > theta, 1, jnp.where(key == theta, jnp.where(j * KT + kloc < jlim, 1, 0), 0))
        key_ref[j] = jnp.where(key > _KEY_NEG_INF, sel, 0)
        return carry

    lax.fori_loop(0, nt, finalize, 0)

    m_ref[...] = jnp.full(m_ref.shape, MAX_INIT, F32)
    l_ref[...] = jnp.zeros(l_ref.shape, F32)
    acc_ref[...] = jnp.zeros(acc_ref.shape, F32)

    def att_tile(j, carry):
        off = pl.multiple_of(j * KT, KT)
        bias_ref[...] = jnp.where(key_ref[j] > 0, 0.0, NEG_BIG)
        for h in range(N_HEADS):
            sl = slice(h * HEAD_DIM, (h + 1) * HEAD_DIM)
            s_ref[h] = _dot(k_ref[pl.ds(off, KT), sl], qt_ref[sl, :]) + bias_ref[...]
        for h in range(N_HEADS):
            m_old = m_ref[h:h + 1, :]
            m_new = jnp.maximum(m_old, jnp.max(s_ref[h], axis=0, keepdims=True))
            p = jnp.exp(s_ref[h] - m_new)
            alpha = jnp.exp(m_old - m_new)
            l_ref[h:h + 1, :] = alpha * l_ref[h:h + 1, :] + jnp.sum(p, axis=0, keepdims=True)
            p_ref[h] = p.astype(BF16)
            alpha_ref[h:h + 1, :] = alpha
            m_ref[h:h + 1, :] = m_new
        for h in range(N_HEADS):
            sl = slice(h * HEAD_DIM, (h + 1) * HEAD_DIM)
            acc_ref[sl, :] = alpha_ref[h:h + 1, :] * acc_ref[sl, :] + _dot(vt_ref[j, sl, :], p_ref[h])
        return carry

    lax.fori_loop(0, nt, att_tile, 0)
    for h in range(N_HEADS):
        sl = slice(h * HEAD_DIM, (h + 1) * HEAD_DIM)
        o_ref[:, sl] = (acc_ref[sl, :] / l_ref[h:h + 1, :]).T.astype(BF16)


def dsa_prompt(qt, iqt, misct, miscr, kb, vt, *, B, S):
    QB = KEY_TILE
    topk = min(TOPK_MAX, S // 4)
    nq = S // QB
    col = lambda w: pl.BlockSpec((w, QB), lambda b, i: (0, b * nq + i))
    return pl.pallas_call(
        functools.partial(_dsa_prompt_body, topk=topk, nbits=S.bit_length()),
        grid=(B, nq),
        in_specs=[col(A_W), col(IDX_QW), col(MISC_W),
                  pl.BlockSpec((S, A_W), lambda b, i: (b, 0)),
                  pl.BlockSpec((nq, A_W, KEY_TILE), lambda b, i: (b, 0, 0)),
                  pl.BlockSpec((S, MISC_W), lambda b, i: (b, 0))],
        out_specs=pl.BlockSpec((QB, A_W), lambda b, i: (b * nq + i, 0)),
        out_shape=jax.ShapeDtypeStruct((B * S, A_W), BF16),
        scratch_shapes=[pltpu.VMEM((nq, KEY_TILE, QB), I32), pltpu.VMEM((1, QB), I32),
                        pltpu.VMEM((N_HEADS, QB), F32), pltpu.VMEM((N_HEADS, QB), F32), pltpu.VMEM((N_HEADS, QB), F32),
                        pltpu.VMEM((A_W, QB), F32), pltpu.VMEM((KEY_TILE, QB), F32),
                        pltpu.VMEM((N_HEADS, KEY_TILE, QB), F32), pltpu.VMEM((N_HEADS, KEY_TILE, QB), BF16)],
        compiler_params=_cparams(("parallel", "arbitrary")),
        name="dsa_prompt",
    )(qt, iqt, misct, kb, vt, miscr)


PAGES_PER_STEP = 32
WORD_BITS = 32
HALF_BITS = WORD_BITS // 2
WORDS_PER_PAGE = PAGE_SIZE // WORD_BITS
CTZ_MULTIPLIER = np.int32(0x077CB531)
CTZ_TABLE = np.array([0, 1, 28, 2, 29, 14, 24, 3, 30, 22, 20, 15, 25, 17, 4, 8,
                      31, 27, 13, 23, 21, 19, 16, 7, 26, 12, 18, 6, 11, 5, 10, 9], np.int32)
META_COUNT = 0
META_NEW = 1


def _dsa_select_body(pt_ref, iq_ref, coef_ref, ikn_ref, *rest, topk, nbits):
    page_refs = rest[:-3]
    words_ref, meta_ref, sc_ref = rest[-3:]
    G = len(page_refs)
    j = pl.program_id(1)
    iq = iq_ref[...]
    coef = coef_ref[...]
    for r, kref in enumerate(page_refs):
        d = _dot(iq, kref[...].astype(BF16))
        sc_ref[pl.ds(j * G + r, 1), :] = jnp.sum(jnp.maximum(d, 0.0) * coef, axis=0, keepdims=True)

    @pl.when(j == pl.num_programs(1) - 1)
    def _():
        n_pages = sc_ref.shape[0]
        dn = jnp.sum(iq.astype(F32) * ikn_ref[...].astype(F32), axis=1, keepdims=True)
        key_new = _sort_key(jnp.sum(jnp.maximum(dn, 0.0) * coef, axis=0, keepdims=True))
        key = _sort_key(sc_ref[...])
        idx = (lax.broadcasted_iota(I32, key.shape, 0) * PAGE_SIZE + lax.broadcasted_iota(I32, key.shape, 1))
        idx_new = jnp.full((1, 1), n_pages * PAGE_SIZE, I32)

        def count(f):
            past = jnp.sum(jnp.sum(f(key, idx), axis=1, keepdims=True), axis=0, keepdims=True)
            return past + f(key_new, idx_new)

        theta, room, tie_step = _topk_select(count, topk, nbits, (1, 1))
        jlim = lax.fori_loop(0, nbits, tie_step, jnp.zeros((1, 1), I32))
        pick = lambda k, ix: jnp.where(k > theta, 1.0, jnp.where(k == theta, jnp.where(ix < jlim, 1.0, 0.0), 0.0))
        slot = lax.broadcasted_iota(I32, (2 * WORDS_PER_PAGE, PAGE_SIZE), 1)
        row = lax.broadcasted_iota(I32, (2 * WORDS_PER_PAGE, PAGE_SIZE), 0)
        half = jnp.where(row < WORDS_PER_PAGE, 2 * row, 2 * (row - WORDS_PER_PAGE) + 1)
        in_half = lax.shift_right_logical(slot, np.int32(HALF_BITS.bit_length() - 1)) == half
        bit_value = jnp.where(in_half, lax.shift_left(np.int32(1), slot & np.int32(HALF_BITS - 1)), 0)
        picked = pick(key, idx)
        halves = _dot_nt(bit_value.astype(F32).astype(BF16), picked.astype(BF16)).astype(I32)
        words_ref[...] = halves[:WORDS_PER_PAGE] | lax.shift_left(halves[WORDS_PER_PAGE:], np.int32(HALF_BITS))
        n_picked = jnp.sum(jnp.sum(picked, axis=1, keepdims=True), axis=0, keepdims=True)
        lane = lax.broadcasted_iota(I32, meta_ref.shape, 1)
        meta = jnp.where(lane == META_COUNT, n_picked, jnp.where(lane == META_NEW, pick(key_new, idx_new), 0.0))
        meta_ref[...] = meta.astype(I32)


def dsa_sample_select(iq3, coef3, ikb3, cache_kidx, page_table, layer):
    B, n_pages = page_table.shape
    G = min(PAGES_PER_STEP, n_pages)
    n_keys = n_pages * PAGE_SIZE + 1
    topk = min(TOPK_MAX, n_keys // 4)

    def page_spec(r):
        return pl.BlockSpec((None, None, IDX_DIM, PAGE_SIZE), lambda b, j, pt: (layer, pt[b, j * G + r], 0, 0))

    return pl.pallas_call(
        functools.partial(_dsa_select_body, topk=topk, nbits=n_keys.bit_length()),
        grid_spec=pltpu.PrefetchScalarGridSpec(
            num_scalar_prefetch=1,
            grid=(B, n_pages // G),
            in_specs=[pl.BlockSpec((None, IDX_HEADS, IDX_DIM), lambda b, j, pt: (b, 0, 0)),
                      pl.BlockSpec((None, IDX_HEADS, 1), lambda b, j, pt: (b, 0, 0)),
                      pl.BlockSpec((None, 1, IDX_DIM), lambda b, j, pt: (b, 0, 0))]
                     + [page_spec(r) for r in range(G)],
            out_specs=[pl.BlockSpec((None, WORDS_PER_PAGE, n_pages), lambda b, j, pt: (b, 0, 0)),
                       pl.BlockSpec((None, 1, 128), lambda b, j, pt: (b, 0, 0))],
            scratch_shapes=[pltpu.VMEM((n_pages, PAGE_SIZE), F32)],
        ),
        out_shape=[jax.ShapeDtypeStruct((B, WORDS_PER_PAGE, n_pages), I32),
                   jax.ShapeDtypeStruct((B, 1, 128), I32)],
        compiler_params=_cparams(("parallel", "arbitrary")),
        name="dsa_sample_select",
    )(page_table, iq3, coef3, ikb3, *([cache_kidx] * G))


def _row_copies(ck_ref, cv_ref, kbuf, vbuf, sem, layer, phys, slot, row):
    dst = pl.ds(pl.multiple_of(row * N_HEADS, N_HEADS), N_HEADS)
    return (pltpu.make_async_copy(ck_ref.at[layer, phys, slot], kbuf.at[dst, :], sem.at[0]),
            pltpu.make_async_copy(cv_ref.at[layer, phys, slot], vbuf.at[dst, :], sem.at[1]))


def _dsa_gather_body(pt_ref, ctz_ref, words_ref, meta_ref, q_ref, kn_ref, vn_ref, ck_ref, cv_ref, o_ref,
                     kbuf, vbuf, cnt_ref, sem, *, layer, cap):
    b = pl.program_id(0)
    n_pages = words_ref.shape[-1]
    want = jnp.minimum(meta_ref[0, META_COUNT], cap)
    kbuf[...] = jnp.zeros(kbuf.shape, F32)
    vbuf[...] = jnp.zeros(vbuf.shape, F32)
    cnt_ref[0] = 0

    def scan_page(carry):
        p, _ = carry
        words = [words_ref[w, p] for w in range(WORDS_PER_PAGE)]

        @pl.when(functools.reduce(lambda a, c: a | c, words) != 0)
        def _():
            phys = pt_ref[b, p]
            for w, word in enumerate(words):
                def next_bit(rest, w=w):
                    low = rest & (-rest)
                    bit = ctz_ref[lax.shift_right_logical(low * CTZ_MULTIPLIER, np.int32(27))]
                    n = cnt_ref[0]

                    @pl.when(n < want)
                    def _():
                        for cp in _row_copies(ck_ref, cv_ref, kbuf, vbuf, sem, layer, phys, w * WORD_BITS + bit, n):
                            cp.start()
                        cnt_ref[0] = n + 1

                    return rest ^ low

                lax.while_loop(lambda rest: rest != 0, next_bit, word)

        return p + 1, cnt_ref[0]

    _, cnt = lax.while_loop(lambda c: (c[0] < n_pages) & (c[1] < want), scan_page, (np.int32(0), np.int32(0)))

    def wait_loop(n, carry):
        for cp in _row_copies(ck_ref, cv_ref, kbuf, vbuf, sem, layer, 0, 0, n):
            cp.wait()
        return carry

    lax.fori_loop(0, cnt, wait_loop, 0)

    q8 = q_ref[...]
    valid = lax.broadcasted_iota(I32, (N_HEADS, cap), 1) < cnt
    seln = meta_ref[0, META_NEW] > 0
    sn = jnp.sum(q8.astype(F32) * kn_ref[...].astype(F32), axis=1, keepdims=True)
    vn = vn_ref[...].astype(F32)
    head = lax.broadcasted_iota(I32, (N_HEADS, HEAD_DIM), 0)
    out = jnp.zeros((N_HEADS, HEAD_DIM), F32)
    for h in range(N_HEADS):
        kh = kbuf[pl.ds(h, cap, stride=N_HEADS), :].astype(BF16)
        vh = vbuf[pl.ds(h, cap, stride=N_HEADS), :].astype(BF16)
        s = _dot_nt(q8, kh)
        m = jnp.maximum(jnp.max(jnp.where(valid, s, NEG_BIG), axis=1, keepdims=True), jnp.where(seln, sn, NEG_BIG))
        p = jnp.where(valid, jnp.exp(s - m), 0.0)
        pn = jnp.where(seln, jnp.exp(sn - m), 0.0)
        l = jnp.sum(p, axis=1, keepdims=True) + pn
        o = (_dot(p.astype(BF16), vh) + pn.astype(BF16).astype(F32) * vn) / l
        out = jnp.where(head == h, o, out)
    o_ref[...] = out.astype(BF16)


def dsa_sample_gather(words, meta, q8, kn8, vn8, cache_k, cache_v, page_table, layer):
    B, n_pages = page_table.shape
    cap = min(TOPK_MAX, (n_pages * PAGE_SIZE + 1) // 4)
    head_rows = pl.BlockSpec((None, N_HEADS, HEAD_DIM), lambda b, pt, ctz: (b, 0, 0))
    return pl.pallas_call(
        functools.partial(_dsa_gather_body, layer=layer, cap=cap),
        grid_spec=pltpu.PrefetchScalarGridSpec(
            num_scalar_prefetch=2,
            grid=(B,),
            in_specs=[pl.BlockSpec((None, WORDS_PER_PAGE, n_pages), lambda b, pt, ctz: (b, 0, 0),
                                   memory_space=pltpu.SMEM),
                      pl.BlockSpec((None, 1, 128), lambda b, pt, ctz: (b, 0, 0), memory_space=pltpu.SMEM),
                      head_rows, head_rows, head_rows,
                      pl.BlockSpec(memory_space=pl.ANY), pl.BlockSpec(memory_space=pl.ANY)],
            out_specs=head_rows,
            scratch_shapes=[pltpu.VMEM((cap * N_HEADS, HEAD_DIM), F32), pltpu.VMEM((cap * N_HEADS, HEAD_DIM), F32),
                            pltpu.SMEM((1,), I32), pltpu.SemaphoreType.DMA((2,))],
        ),
        out_shape=jax.ShapeDtypeStruct((B, N_HEADS, HEAD_DIM), BF16),
        compiler_params=_cparams(("arbitrary",)),
        name="dsa_sample_gather",
    )(page_table, jnp.asarray(CTZ_TABLE), words, meta, q8, kn8, vn8, cache_k, cache_v)


def _merge_body(x_ref, hm_ref, att_ref, ga_ref, gb_ref, wa_ref, wb_ref, wo_ref, o_ref):
    br_a = _dot(hm_ref[...], wa_ref[...])
    br_b = _dot(att_ref[...], wb_ref[...])
    t = jax.nn.sigmoid(ga_ref[...].astype(F32)) * br_a + jax.nn.sigmoid(gb_ref[...].astype(F32)) * br_b
    o_ref[...] = x_ref[...] + _dot(t.astype(BF16), wo_ref[...])


def merge(x, hm, att, zg, wa_all, wb_all, wo_all, layer, *, tm):
    M, D = x.shape
    wspec = lambda k: pl.BlockSpec((None, k, D), lambda i: (layer, 0, 0))
    return pl.pallas_call(
        _merge_body,
        grid=(M // tm,),
        in_specs=[pl.BlockSpec((tm, D), lambda i: (i, 0)),
                  pl.BlockSpec((tm, M_V), lambda i: (i, 0)),
                  pl.BlockSpec((tm, A_W), lambda i: (i, 0)),
                  pl.BlockSpec((tm, D), lambda i: (i, C_GA // D)),
                  pl.BlockSpec((tm, D), lambda i: (i, C_GB // D)),
                  wspec(M_V), wspec(A_W), wspec(D)],
        out_specs=pl.BlockSpec((tm, D), lambda i: (i, 0)),
        out_shape=jax.ShapeDtypeStruct((M, D), F32),
        compiler_params=_cparams(("parallel",)),
        name="merge",
    )(x, hm, att, zg, zg, wa_all, wb_all, wo_all)


S_MQ, S_MK, S_MV, S_MO, S_MI, S_MF = 0, 512, 1024, 2048, 3072, 3076
S_AQ, S_AK, S_AV, S_IQ, S_IK, S_IW, S_GA, S_GB, S_END = 3080, 4104, 5128, 6152, 7176, 7240, 7256, 9304, 11352
G_SOURCES = ((S_GA, D_MODEL), (S_GB, D_MODEL), (S_MQ, M_QK), (S_MK, M_QK), (S_MV, M_V), (S_MO, M_V))
A_SOURCES = ((S_AQ, A_W), (S_AK, A_W), (S_AV, A_W), (S_IQ, IDX_QW))


def _relayout_body(r_tab, t_tab, src_ref, tail_ref, o_ref):
    is_tail = t_tab[pl.program_id(1)]

    @pl.when(is_tail == 0)
    def _():
        o_ref[...] = src_ref[0].T.astype(BF16)

    @pl.when(is_tail != 0)
    def _():
        o_ref[...] = tail_ref[...].T.astype(BF16)


def _relayout(w_t, sources, tail):
    depth, _, D = w_t.shape
    W = RELAYOUT_W
    starts = [c0 + k * W for c0, width in sources for k in range(width // W)]
    assert all(c % 8 == 0 for c in starts)
    r_tab = [c // 8 for c in starts] + [0] * (tail is not None)
    t_tab = [0] * len(starts) + [1] * (tail is not None)
    if tail is None:
        tail = jnp.zeros((depth, W, D), F32)
    n = len(r_tab)
    tabs = [jnp.asarray(np.array(t, np.int32)) for t in (r_tab, t_tab)]
    return pl.pallas_call(
        _relayout_body,
        grid_spec=pltpu.PrefetchScalarGridSpec(
            num_scalar_prefetch=2,
            grid=(depth, n),
            in_specs=[pl.BlockSpec((pl.Element(1), pl.Element(W), pl.Element(D)), lambda l, j, r, t: (l, r[j] * 8, 0)),
                      pl.BlockSpec((None, W, D), lambda l, j, r, t: (l, 0, 0))],
            out_specs=pl.BlockSpec((None, D, W), lambda l, j, r, t: (l, 0, j)),
        ),
        out_shape=jax.ShapeDtypeStruct((depth, D, n * W), BF16),
        compiler_params=_cparams(("parallel", "arbitrary")),
        name="relayout_w_in",
    )(*tabs, w_t, tail)


def relayout_w_in(w_in, b_in):
    def small(a, axis):
        take = lambda lo, hi: lax.slice_in_dim(a, lo, hi, axis=axis)
        pad_shape = list(a.shape)
        pad_shape[axis] = RELAYOUT_W - (L_MF + M_HEADS)
        return jnp.concatenate([take(S_IK, S_GA), take(S_MI, S_AQ), jnp.zeros(pad_shape, a.dtype)], axis=axis)

    take = lambda a, sources: [a[..., c0:c0 + width] for c0, width in sources]
    bg = jnp.concatenate(take(b_in, G_SOURCES), axis=-1)
    ba = jnp.concatenate(take(b_in, A_SOURCES) + [small(b_in, 1)], axis=-1)
    w_t = jnp.swapaxes(w_in, 1, 2)
    wg = _relayout(w_t, G_SOURCES, None)
    wa = _relayout(w_t, A_SOURCES, small(w_t, 1))
    return wg, wa, bg[:, None, :], ba[:, None, :]


FFN_ROWS = 1024
FFN_COLS = 256
INPROJ_G_COLS = 1024
INPROJ_A_ROWS = 1024
INPROJ_A_COLS = NZ_A // 3
POST_ROWS = 512
MLSTM_CHUNK = 256
SAMPLE_PAD = 16


def _mixer_block(x, layer, P, tabs, kv_rows, *, depth, tm, ta, tq, g_dtype, tab_blocks, transposed, mixers):
    zg = inproj(x, P["g_mix"], P["wg"], P["bg"], layer, tm=tm, tn=INPROJ_G_COLS, out_dtype=g_dtype)
    za = inproj(x, P["g_mix"], P["wa"], P["ba"], layer, tm=ta, tn=INPROJ_A_COLS, out_dtype=F32)
    post = qkpost(za, P["g_q"], P["g_k"], P["g_ik"], tabs, layer, kv_rows, depth=depth, tm=tq,
                  tab_blocks=tab_blocks, transposed=transposed)
    kv_rows, miscr = (post[1], post[3]), post[6]
    hm, att, state = mixers(zg, za, post)
    x = merge(x, hm, att, zg, P["w_a"], P["w_b"], P["w_o"], layer, tm=tq)
    return x, kv_rows, (miscr[:, :IDX_DIM],) + state


def kernel(x_prompt, x_sample, cache_k, cache_v, cache_kidx, state_C, state_n, state_m, page_table, norm_ffn1, w_ffn1_gu, w_ffn1_down, norm_mix, w_in, b_in, q_norm, k_norm, idx_k_norm, mlstm_norm, w_branch_a, w_branch_b, w_out, norm_ffn2, w_ffn2_gu, w_ffn2_down):
    B, S, D = x_prompt.shape
    BS = x_sample.shape[0]
    depth = w_in.shape[0]
    past = page_table.shape[1] * PAGE_SIZE

    wg, wa, bg, ba = relayout_w_in(w_in, b_in)
    pad128 = lambda g: jnp.pad(g, ((0, 0), (0, 128 - g.shape[-1])))[:, None, :]
    P = dict(g_f1=norm_ffn1[:, None, :], w_f1_gu=w_ffn1_gu, w_f1_dn=w_ffn1_down,
             g_mix=norm_mix[:, None, :], wg=wg, wa=wa, bg=bg, ba=ba,
             g_q=q_norm[:, None, :], g_k=k_norm[:, None, :], g_ik=pad128(idx_k_norm),
             g_m=mlstm_norm.reshape(depth, 1, M_V),
             w_a=w_branch_a.astype(BF16), w_b=w_branch_b.astype(BF16), w_o=w_out.astype(BF16),
             g_f2=norm_ffn2[:, None, :], w_f2_gu=w_ffn2_gu, w_f2_dn=w_ffn2_down)
    tabs_p = rope_tables(jnp.arange(S))
    tabs_s = rope_tables(jnp.full((BS,), past, jnp.int32))
    kidx_t = jnp.swapaxes(cache_kidx, 2, 3)

    xp = x_prompt.reshape(B * S, D)
    xs = x_sample.reshape(BS, D)
    pad_rows = lambda a: jnp.pad(a, ((0, SAMPLE_PAD - BS), (0, 0)))
    tm = min(FFN_ROWS, B * S)
    ta = min(INPROJ_A_ROWS, B * S)
    tq = min(POST_ROWS, B * S)
    rows_p, rows_s = [], []
    kv_p = tuple(jnp.zeros((depth, B * S, N_HEADS, HEAD_DIM), F32) for _ in range(2))
    kv_s = tuple(jnp.zeros((depth, BS, N_HEADS, HEAD_DIM), F32) for _ in range(2))
    for layer in range(depth):
        def mix_prompt(zg, za, post):
            qt, _, kb, _, vt, iqt, miscr, misct = post
            hm, C, n, m = mlstm_prompt(zg, za, P["g_m"], layer, B=B, S=S, T=MLSTM_CHUNK)
            att = dsa_prompt(qt, iqt, misct, miscr, kb, vt, B=B, S=S)
            return hm, att, (C, n, m)

        def mix_sample(zg, za, post):
            qb, _, kb, _, vb, iqb, miscr = post
            hm, C, n, m = mlstm_step(zg, za, P["g_m"], state_C, state_n, state_m, layer)
            iq3 = iqb.reshape(BS, IDX_HEADS, IDX_DIM)
            coef3 = miscr[:, L_IW:L_IW + IDX_HEADS].reshape(BS, IDX_HEADS, 1)
            ikb3 = miscr[:, :IDX_DIM].astype(BF16).reshape(BS, 1, IDX_DIM)
            words, meta = dsa_sample_select(iq3, coef3, ikb3, kidx_t, page_table, layer)
            heads = lambda a: a.reshape(BS, N_HEADS, HEAD_DIM)
            att = dsa_sample_gather(words, meta, heads(qb), heads(kb), heads(vb), cache_k, cache_v, page_table, layer)
            return hm, att.reshape(BS, A_W), (C, n, m)

        xp, xs16 = ffn(xp, pad_rows(xs), P["g_f1"], P["w_f1_gu"], P["w_f1_dn"], layer, tm=tm, tf=FFN_COLS)
        xp, kv_p, rp = _mixer_block(xp, layer, P, tabs_p, kv_p, depth=depth, tm=tm, ta=ta, tq=tq, g_dtype=BF16,
                                    tab_blocks=S // tq, transposed=True, mixers=mix_prompt)
        xs, kv_s, rs = _mixer_block(xs16[:BS], layer, P, tabs_s, kv_s, depth=depth, tm=BS, ta=BS, tq=BS, g_dtype=F32,
                                    tab_blocks=1, transposed=False, mixers=mix_sample)
        xp, xs16 = ffn(xp, pad_rows(xs), P["g_f2"], P["w_f2_gu"], P["w_f2_dn"], layer, tm=tm, tf=FFN_COLS)
        xs = xs16[:BS]
        rows_p.append(rp)
        rows_s.append(rs)

    def stack(rows, idx, shape):
        return jnp.stack([r[idx] for r in rows]).reshape((depth,) + shape)

    return (xp.reshape(B, S, D), xs.reshape(BS, 1, D),
            kv_p[0].reshape(depth, B, S, N_HEADS, HEAD_DIM), kv_p[1].reshape(depth, B, S, N_HEADS, HEAD_DIM),
            stack(rows_p, 0, (B, S, IDX_DIM)),
            stack(rows_p, 1, (B, M_HEADS, M_QK_DIM, M_V_DIM)), stack(rows_p, 2, (B, M_HEADS, M_QK_DIM)),
            stack(rows_p, 3, (B, M_HEADS)),
            kv_s[0].reshape(depth, BS, 1, N_HEADS, HEAD_DIM), kv_s[1].reshape(depth, BS, 1, N_HEADS, HEAD_DIM),
            stack(rows_s, 0, (BS, 1, IDX_DIM)),
            stack(rows_s, 1, (BS, M_HEADS, M_QK_DIM, M_V_DIM)), stack(rows_s, 2, (BS, M_HEADS, M_QK_DIM)),
            stack(rows_s, 3, (BS, M_HEADS)))
```
